```python
import jax
import jax.numpy as jnp
from jax import lax
import numpy as np

D_MODEL = 1024
BATCH = 8
SEQ = 4096
DEPTH = 4

HEAD_DIM = 64
N_GROUP_HEADS = D_MODEL // (4 * HEAD_DIM)
N_SB_HEADS = N_GROUP_HEADS
N_GLA_HEADS = N_GROUP_HEADS
N_DIL_HEADS = N_GROUP_HEADS
N_FOX_HEADS = N_GROUP_HEADS
N_MIX_HEADS = N_SB_HEADS + N_GLA_HEADS + N_DIL_HEADS + N_FOX_HEADS
GLA_KEY_DIM = HEAD_DIM // 2
GLA_GATE_RANK = 16
GLA_GATE_TAU = 16.0
GLA_CHUNK = 16
DIL_PAIRS = ((128, 1), (512, 4), (2048, 16))
DIL_BLOCK = 128
Q_BLOCK = 128
PEER_HEADS = 8
PEER_N_KEYS = 128
PEER_N_EXPERTS = PEER_N_KEYS * PEER_N_KEYS
PEER_QUERY_DIM = 256
PEER_TOPK = 16
PEER_TOKEN_BLOCK = 128
NORM_EPS = 1e-6

GROUP_W = N_GROUP_HEADS * HEAD_DIM
GLA_QK_W = N_GLA_HEADS * GLA_KEY_DIM
IN_SPLITS = (GROUP_W, GROUP_W, GROUP_W,
             GLA_QK_W, GLA_QK_W, GROUP_W, GROUP_W, GLA_GATE_RANK,
             GROUP_W, GROUP_W, GROUP_W,
             GROUP_W, GROUP_W, GROUP_W, N_FOX_HEADS)
PROJ_W = sum(IN_SPLITS)

kernel_name = 'hybrid_sb_gla_dilated_fox_peer_trunk'


def rms_norm(x, g):
    xf = x.astype(jnp.float32)
    y = xf * lax.rsqrt(jnp.mean(xf * xf, axis=-1, keepdims=True) + NORM_EPS)
    return (y * g.astype(jnp.float32)).astype(x.dtype)


def head_rms_norm(o, g):
    of = o.astype(jnp.float32)
    y = of * lax.rsqrt(jnp.mean(of * of, axis=-1, keepdims=True) + NORM_EPS)
    return y * g.astype(jnp.float32)[None, :, None, :]


def to_heads(t, n):
    b, s, _ = t.shape
    return t.reshape(b, s, n, -1).transpose(0, 2, 1, 3)


def split_columns(p):
    out = []
    start = 0
    for w in IN_SPLITS:
        out.append(p[..., start:start + w])
        start += w
    return out


def to_query_blocks(t):
    b, h, s, d = t.shape
    return t.reshape(b, h, s // Q_BLOCK, Q_BLOCK, d).transpose(2, 0, 1, 3, 4)


def from_query_blocks(o):
    nb, b, h, qb, d = o.shape
    return o.transpose(1, 2, 0, 3, 4).reshape(b, h, nb * qb, d)


def stick_breaking_attention(q, k, v):
    b, h, s, dh = q.shape
    nb = s // Q_BLOCK
    scale = dh ** -0.5
    kpos = jnp.arange(s)

    def block(args):
        qblk, i = args
        qpos = i * Q_BLOCK + jnp.arange(Q_BLOCK)
        z = jnp.einsum('bhqd,bhkd->bhqk', qblk, k).astype(jnp.float32) * scale
        mask = kpos[None, :] < qpos[:, None]
        log_beta = jax.nn.log_sigmoid(z)
        log_one_minus = jnp.where(mask, jax.nn.log_sigmoid(-z), 0.0)
        tail = lax.cumsum(log_one_minus, axis=3, reverse=True) - log_one_minus
        w = jnp.where(mask, jnp.exp(log_beta + tail), 0.0)
        return jnp.einsum('bhqk,bhkd->bhqd', w.astype(v.dtype), v)

    out = lax.map(block, (to_query_blocks(q), jnp.arange(nb)))
    return from_query_blocks(out)


def forgetting_attention(q, k, v, log_f):
    b, h, s, dh = q.shape
    nb = s // Q_BLOCK
    scale = dh ** -0.5
    cum_f = jnp.cumsum(log_f, axis=-1)
    cum_f_blocks = cum_f.reshape(b, h, nb, Q_BLOCK).transpose(2, 0, 1, 3)
    kpos = jnp.arange(s)

    def block(args):
        qblk, fq, i = args
        qpos = i * Q_BLOCK + jnp.arange(Q_BLOCK)
        logits = (jnp.einsum('bhqd,bhkd->bhqk', qblk, k).astype(jnp.float32) * scale
                  + fq[..., None] - cum_f[:, :, None, :])
        logits = jnp.where(kpos[None, :] <= qpos[:, None], logits, -jnp.inf)
        p = jax.nn.softmax(logits, axis=-1)
        return jnp.einsum('bhqk,bhkd->bhqd', p.astype(v.dtype), v)

    out = lax.map(block, (to_query_blocks(q), cum_f_blocks, jnp.arange(nb)))
    return from_query_blocks(out)


def gla_chunked(q, k, v, log_a):
    b, h, s, dk = q.shape
    dv = v.shape[-1]
    c = GLA_CHUNK
    n = s // c
    qc = q.astype(jnp.float32).reshape(b, h, n, c, dk) * dk ** -0.5
    kc = k.astype(jnp.float32).reshape(b, h, n, c, dk)
    vc = v.astype(jnp.float32).reshape(b, h, n, c, dv)
    g = jnp.cumsum(log_a.reshape(b, h, n, c, dk), axis=3)
    g_last = g[:, :, :, -1:, :]
    causal = jnp.tril(jnp.ones((c, c), dtype=bool))[:, :, None]
    diff = g[:, :, :, :, None, :] - g[:, :, :, None, :, :]
    decay = jnp.exp(jnp.where(causal, diff, -jnp.inf))
    scores = jnp.einsum('bhntk,bhnsk,bhntsk->bhnts', qc, kc, decay)
    o_intra = jnp.einsum('bhnts,bhnsv->bhntv', scores, vc)
    q_in = qc * jnp.exp(g)
    k_out = kc * jnp.exp(g_last - g)
    a_chunk = jnp.exp(g_last[:, :, :, 0, :])

    def step(state, inp):
        qi, ki, vi, ai = inp
        o = jnp.einsum('bhtk,bhkv->bhtv', qi, state)
        state = ai[..., None] * state + jnp.einsum('bhsk,bhsv->bhkv', ki, vi)
        return state, o

    state0 = jnp.zeros((b, h, dk, dv), jnp.float32)
    xs = (jnp.moveaxis(q_in, 2, 0), jnp.moveaxis(k_out, 2, 0), jnp.moveaxis(vc, 2, 0), jnp.moveaxis(a_chunk, 2, 0))
    _, o_inter = lax.scan(step, state0, xs)
    o = o_intra + jnp.moveaxis(o_inter, 0, 2)
    return o.reshape(b, h, s, dv)


def alibi_slopes(n):
    return 2.0 ** (-8.0 * jnp.arange(1, n + 1, dtype=jnp.float32) / n)


def dilated_branch(q, k, v, dilation, n_steps, slopes):
    b, h, s, dh = q.shape
    r = dilation
    m = s // r
    blk = DIL_BLOCK
    nb = -(-m // blk)
    mp = nb * blk

    def regroup(t):
        return t.reshape(b, h, m, r, dh).transpose(0, 1, 3, 2, 4)

    qs = jnp.pad(regroup(q), ((0, 0), (0, 0), (0, 0), (0, mp - m), (0, 0))).reshape(b, h, r, nb, blk, dh)

    def key_windows(t):
        tp = jnp.pad(regroup(t), ((0, 0), (0, 0), (0, 0), (blk, mp - m), (0, 0))).reshape(b, h, r, nb + 1, blk, dh)
        return jnp.concatenate([tp[:, :, :, :-1], tp[:, :, :, 1:]], axis=4)

    kw = key_windows(k)
    vw = key_windows(v)
    logits = jnp.einsum('bhrnqd,bhrnkd->bhrnqk', qs, kw).astype(jnp.float32) * dh ** -0.5
    qi = jnp.arange(blk)[:, None]
    kj = jnp.arange(2 * blk)[None, :]
    steps = qi + blk - kj
    key_sub = jnp.arange(nb)[:, None, None] * blk + kj[None] - blk
    valid = (steps >= 0) & (steps <= n_steps) & (key_sub >= 0)
    bias = -slopes[:, None, None] * (steps * r).astype(jnp.float32)
    logits = jnp.where(valid, logits + bias[None, :, None, None], -jnp.inf)
    mx = jnp.max(logits, axis=-1)
    p = jnp.exp(logits - mx[..., None])
    den = jnp.sum(p, axis=-1)
    num = jnp.einsum('bhrnqk,bhrnkd->bhrnqd', p, vw.astype(jnp.float32))

    def ungroup(t):
        extra = t.shape[5:]
        t = t.reshape((b, h, r, mp) + extra)[:, :, :, :m]
        return jnp.moveaxis(t, 2, 3).reshape((b, h, s) + extra)

    return ungroup(num), ungroup(den), ungroup(mx)


def dilated_attention(q, k, v, slopes):
    parts = [dilated_branch(q, k, v, r, w // r, slopes) for (w, r) in DIL_PAIRS]
    mx_all = parts[0][2]
    for part in parts[1:]:
        mx_all = jnp.maximum(mx_all, part[2])
    num = 0.0
    den = 0.0
    for (nu, de, mx) in parts:
        wgt = jnp.exp(mx - mx_all)
        num = num + wgt[..., None] * nu
        den = den + wgt * de
    return num / den[..., None]


def peer_ffn(h, w_query, sub_keys, expert_u, expert_v):
    b, s, d = h.shape
    half = PEER_QUERY_DIM // 2
    tb = PEER_TOKEN_BLOCK
    tokens = h.reshape(-1, tb, d)

    def block(xb):
        qry = jnp.einsum('td,dq->tq', xb, w_query).reshape(tb, PEER_HEADS, 2, half)
        scores = jnp.einsum('thcd,hcnd->thcn', qry, sub_keys).astype(jnp.float32)
        s_top, i_top = lax.top_k(scores, PEER_TOPK)
        cand = s_top[:, :, 0, :, None] + s_top[:, :, 1, None, :]
        cidx = i_top[:, :, 0, :, None] * PEER_N_KEYS + i_top[:, :, 1, None, :]
        best, pos = lax.top_k(cand.reshape(tb, PEER_HEADS, PEER_TOPK * PEER_TOPK), PEER_TOPK)
        eidx = jnp.take_along_axis(cidx.reshape(tb, PEER_HEADS, PEER_TOPK * PEER_TOPK), pos, axis=-1)
        gate = jax.nn.softmax(best, axis=-1)
        u = expert_u[eidx]
        v = expert_v[eidx]
        act = jax.nn.gelu(jnp.einsum('td,thkd->thk', xb, u).astype(jnp.float32), approximate=False)
        return jnp.einsum('thk,thkd->td', (gate * act).astype(v.dtype), v)

    out = lax.map(block, tokens)
    return out.reshape(b, s, d)


def setup_inputs(seed: int = 0) -> dict:
    key = jax.random.key(seed)
    ks = jax.random.split(key, 14)
    f32 = jnp.float32
    x = jax.random.normal(ks[0], (BATCH, SEQ, D_MODEL), f32)
    norm_mix_g = 1.0 + 0.02 * jax.random.normal(ks[1], (DEPTH, D_MODEL), f32)
    w_in = jax.random.normal(ks[2], (DEPTH, D_MODEL, PROJ_W), f32) * D_MODEL ** -0.5
    w_gla_up = jax.random.normal(ks[3], (DEPTH, GLA_GATE_RANK, GLA_QK_W), f32) * GLA_GATE_RANK ** -0.5
    b_gla = 0.1 * jax.random.normal(ks[4], (DEPTH, GLA_QK_W), f32)
    b_fox = 3.0 + 0.1 * jax.random.normal(ks[5], (DEPTH, N_FOX_HEADS), f32)
    head_norm_g = 1.0 + 0.02 * jax.random.normal(ks[6], (DEPTH, N_MIX_HEADS, HEAD_DIM), f32)
    w_out = jax.random.normal(ks[7], (DEPTH, N_MIX_HEADS * HEAD_DIM, D_MODEL), f32) * (N_MIX_HEADS * HEAD_DIM) ** -0.5
    norm_ffn_g = 1.0 + 0.02 * jax.random.normal(ks[8], (DEPTH, D_MODEL), f32)
    w_peer_q = jax.random.normal(ks[9], (DEPTH, D_MODEL, PEER_HEADS * PEER_QUERY_DIM), f32) * D_MODEL ** -0.5
    peer_sub_keys = jax.random.normal(ks[10], (DEPTH, PEER_HEADS, 2, PEER_N_KEYS, PEER_QUERY_DIM // 2), f32) * (PEER_QUERY_DIM // 2) ** -0.5
    peer_u = jax.random.normal(ks[11], (DEPTH, PEER_N_EXPERTS, D_MODEL), f32) * D_MODEL ** -0.5
    peer_v = 0.25 * jax.random.normal(ks[12], (DEPTH, PEER_N_EXPERTS, D_MODEL), f32)
    norm_final_g = 1.0 + 0.02 * jax.random.normal(ks[13], (D_MODEL,), f32)
    return {'x': x, 'norm_mix_g': norm_mix_g, 'w_in': w_in, 'w_gla_up': w_gla_up, 'b_gla': b_gla,
            'b_fox': b_fox, 'head_norm_g': head_norm_g, 'w_out': w_out, 'norm_ffn_g': norm_ffn_g,
            'w_peer_q': w_peer_q, 'peer_sub_keys': peer_sub_keys, 'peer_u': peer_u, 'peer_v': peer_v,
            'norm_final_g': norm_final_g}


def reference(x, norm_mix_g, w_in, w_gla_up, b_gla, b_fox, head_norm_g, w_out, norm_ffn_g,
              w_peer_q, peer_sub_keys, peer_u, peer_v, norm_final_g):
    b, s, _ = x.shape
    slopes = alibi_slopes(N_DIL_HEADS)
    h0 = N_SB_HEADS
    h1 = h0 + N_GLA_HEADS
    h2 = h1 + N_DIL_HEADS
    for l in range(DEPTH):
        hn = rms_norm(x, norm_mix_g[l])
        proj = jnp.einsum('bsd,dp->bsp', hn, w_in[l])
        (sb_q, sb_k, sb_v, gla_q, gla_k, gla_v, gla_r, gla_gd,
         dil_q, dil_k, dil_v, fox_q, fox_k, fox_v, fox_f) = split_columns(proj)
        g = head_norm_g[l]

        o_sb = stick_breaking_attention(to_heads(sb_q, N_SB_HEADS), to_heads(sb_k, N_SB_HEADS),
                                        to_heads(sb_v, N_SB_HEADS))
        o_sb = head_rms_norm(o_sb, g[:h0])

        gate_logit = (jnp.einsum('bsr,rk->bsk', gla_gd, w_gla_up[l]) + b_gla[l]).astype(jnp.float32)
        log_a = jax.nn.log_sigmoid(gate_logit) / GLA_GATE_TAU
        o_gla = gla_chunked(to_heads(gla_q, N_GLA_HEADS), to_heads(gla_k, N_GLA_HEADS),
                            to_heads(gla_v, N_GLA_HEADS), to_heads(log_a, N_GLA_HEADS))
        o_gla = head_rms_norm(o_gla, g[h0:h1]) * jax.nn.silu(to_heads(gla_r, N_GLA_HEADS).astype(jnp.float32))

        o_dil = dilated_attention(to_heads(dil_q, N_DIL_HEADS), to_heads(dil_k, N_DIL_HEADS),
                                  to_heads(dil_v, N_DIL_HEADS), slopes)
        o_dil = head_rms_norm(o_dil, g[h1:h2])

        log_f = jax.nn.log_sigmoid(fox_f.astype(jnp.float32) + b_fox[l].astype(jnp.float32))
        o_fox = forgetting_attention(to_heads(fox_q, N_FOX_HEADS), to_heads(fox_k, N_FOX_HEADS),
                                     to_heads(fox_v, N_FOX_HEADS), log_f.transpose(0, 2, 1))
        o_fox = head_rms_norm(o_fox, g[h2:])

        o = jnp.concatenate([o_sb, o_gla, o_dil, o_fox], axis=1)
        o = o.transpose(0, 2, 1, 3).reshape(b, s, N_MIX_HEADS * HEAD_DIM).astype(x.dtype)
        x = x + jnp.einsum('bsm,md->bsd', o, w_out[l])
        x = x + peer_ffn(rms_norm(x, norm_ffn_g[l]), w_peer_q[l], peer_sub_keys[l], peer_u[l], peer_v[l])
    return rms_norm(x, norm_final_g)
```

```python
import functools
import math

import numpy as np
import jax
import jax.numpy as jnp
from jax import lax
from jax.experimental import pallas as pl
from jax.experimental.pallas import tpu as pltpu

F32 = jnp.float32
BF16 = jnp.bfloat16

D_MODEL = 1024
HEAD_DIM = 64
N_GROUP_HEADS = 4
GROUP_W = N_GROUP_HEADS * HEAD_DIM
GLA_KEY_DIM = 32
GLA_QK_W = N_GROUP_HEADS * GLA_KEY_DIM
GLA_GATE_RANK = 16
GLA_GATE_TAU = 16.0
GLA_CHUNK = 16
DIL_PAIRS = ((128, 1), (512, 4), (2048, 16))
PEER_HEADS = 8
PEER_N_KEYS = 128
PEER_N_EXPERTS = PEER_N_KEYS * PEER_N_KEYS
PEER_HALF = 128
PEER_TOPK = 16
NORM_EPS = 1e-6

LANES = 128
MXU_DIM = 256
VMEM_LIMIT = 56 * 1024 * 1024

NEG = -1e30

C_SB = 0
C_FOX = 768
C_DIL = 1536
C_GLA_QK = 2304
C_GLA_V = 2560
C_GLA_R = 2816
MAIN_W = 3072
AUX_GD = 0
AUX_FOX = 16


def _cparams(sem):
    return pltpu.CompilerParams(dimension_semantics=sem, vmem_limit_bytes=VMEM_LIMIT)


def _split3(x):
    a = x.astype(BF16)
    r = x - a.astype(F32)
    b = r.astype(BF16)
    c = (r - b.astype(F32)).astype(BF16)
    return a, b, c


def _dot(a, b):
    return jnp.dot(a, b, preferred_element_type=F32)


def _dot_nt(a, b):
    return lax.dot_general(a, b, (((1,), (1,)), ((), ())), preferred_element_type=F32)


def _dot_tn(a, b):
    return lax.dot_general(a, b, (((0,), (0,)), ((), ())), preferred_element_type=F32)


def _rms(xf, g):
    ms = jnp.mean(xf * xf, axis=-1, keepdims=True)
    return xf * lax.rsqrt(ms + NORM_EPS) * g


def _mix_in_kernel(x_ref, g_ref, w_ref, wah_ref, wal_ref, main_ref, aux_ref, *, n_chunk):
    y = _rms(x_ref[...], g_ref[...])
    y_hi = y.astype(BF16)
    y_lo = (y - y_hi.astype(F32)).astype(BF16)
    cw = MAIN_W // n_chunk
    for c in range(n_chunk):
        main_ref[:, c * cw:(c + 1) * cw] = _dot(y_hi, w_ref[:, c * cw:(c + 1) * cw]).astype(BF16)
    wah = wah_ref[...]
    aux_ref[...] = _dot(y_hi, wah) + _dot(y_hi, wal_ref[...]) + _dot(y_lo, wah)


def _mix_in(x2, g, w_main, wa_hi, wa_lo, tm=512):
    t = x2.shape[0]
    return pl.pallas_call(
        functools.partial(_mix_in_kernel, n_chunk=6),
        grid=(t // tm,),
        in_specs=[
            pl.BlockSpec((tm, D_MODEL), lambda i: (i, 0)),
            pl.BlockSpec((1, D_MODEL), lambda i: (0, 0)),
            pl.BlockSpec((D_MODEL, MAIN_W), lambda i: (0, 0)),
            pl.BlockSpec((D_MODEL, LANES), lambda i: (0, 0)),
            pl.BlockSpec((D_MODEL, LANES), lambda i: (0, 0)),
        ],
        out_specs=[
            pl.BlockSpec((tm, MAIN_W), lambda i: (i, 0)),
            pl.BlockSpec((tm, LANES), lambda i: (i, 0)),
        ],
        out_shape=[jax.ShapeDtypeStruct((t, MAIN_W), BF16), jax.ShapeDtypeStruct((t, LANES), F32)],
        compiler_params=_cparams(("parallel",)),
        name="mix_in",
    )(x2, g, w_main, wa_hi, wa_lo)


def _fox_prep_kernel(aux_ref, b_ref, ltri_ref, pq_ref, pk_ref, eq_ref, ek_ref, cq_ref, ck_ref, carry_ref):
    @pl.when(pl.program_id(1) == 0)
    def _():
        carry_ref[...] = jnp.zeros_like(carry_ref)

    lf = jax.nn.log_sigmoid(aux_ref[...] + b_ref[...])
    l1, l2, l3 = _split3(lf)
    ltri = ltri_ref[...]
    cf = _dot(ltri, l1) + _dot(ltri, l2) + _dot(ltri, l3) + carry_ref[0:1, :]
    tp = cf.shape[0]
    carry_ref[...] = jnp.broadcast_to(cf[tp - 1:tp, :], carry_ref.shape)
    c1, c2, c3 = _split3(cf)
    cq = _dot(c1, pq_ref[0]) + _dot(c2, pq_ref[1]) + _dot(c3, pq_ref[2]) + eq_ref[...]
    ck = _dot(c1, pk_ref[0]) + _dot(c2, pk_ref[1]) + _dot(c3, pk_ref[2]) + ek_ref[...]
    cq_ref[...] = cq.astype(BF16)
    ck_ref[...] = ck.astype(BF16)


def _aug_col(h, m):
    return (h // 2) * LANES + (h % 2) * 8 + m


def _fox_prep_consts():
    pq = np.zeros((3, LANES, GROUP_W), np.float32)
    pk = np.zeros((3, LANES, GROUP_W), np.float32)
    eq = np.zeros((1, GROUP_W), np.float32)
    ek = np.zeros((1, GROUP_W), np.float32)
    for h in range(N_GROUP_HEADS):
        for m in range(3):
            pq[m, AUX_FOX + h, _aug_col(h, m)] = 1.0
            ek[0, _aug_col(h, m)] = 1.0
            eq[0, _aug_col(h, 3 + m)] = 1.0
            pk[m, AUX_FOX + h, _aug_col(h, 3 + m)] = -1.0
    return jnp.asarray(pq, BF16), jnp.asarray(pk, BF16), jnp.asarray(eq), jnp.asarray(ek)


def _fox_prep(aux, b_row, bsz, seq, tp=256):
    t = aux.shape[0]
    nt = seq // tp
    ltri = jnp.asarray(np.tril(np.ones((tp, tp), np.float32)), BF16)
    pq, pk, eq, ek = _fox_prep_consts()
    const2 = lambda b, i: (0, 0)
    return pl.pallas_call(
        _fox_prep_kernel,
        grid=(bsz, nt),
        in_specs=[
            pl.BlockSpec((tp, LANES), lambda b, i: (b * nt + i, 0)),
            pl.BlockSpec((1, LANES), const2),
            pl.BlockSpec((tp, tp), const2),
            pl.BlockSpec((3, LANES, GROUP_W), lambda b, i: (0, 0, 0)),
            pl.BlockSpec((3, LANES, GROUP_W), lambda b, i: (0, 0, 0)),
            pl.BlockSpec((1, GROUP_W), const2),
            pl.BlockSpec((1, GROUP_W), const2),
        ],
        out_specs=[
            pl.BlockSpec((tp, GROUP_W), lambda b, i: (b * nt + i, 0)),
            pl.BlockSpec((tp, GROUP_W), lambda b, i: (b * nt + i, 0)),
        ],
        out_shape=[jax.ShapeDtypeStruct((t, GROUP_W), BF16), jax.ShapeDtypeStruct((t, GROUP_W), BF16)],
        scratch_shapes=[pltpu.VMEM((8, LANES), F32)],
        compiler_params=_cparams(("arbitrary", "arbitrary")),
        name="fox_prep",
    )(aux, b_row, ltri, pq, pk, eq, ek)


def _attn_kernel(q_ref, cq_ref, k_ref, ck_ref, v_ref, tab_ref, o_ref, *, tq, n_back, n_tab):
    i = pl.program_id(1)
    lane = lax.broadcasted_iota(jnp.int32, (1, LANES), 1)
    n_prev = i if n_back is None else jnp.minimum(i, n_back)
    zero_b = jnp.zeros((), BF16)

    for p in range(2):
        cs = slice(p * LANES, (p + 1) * LANES)
        qp = q_ref[:, cs]
        cqp = cq_ref[:, cs]
        outs = []
        for e in range(2):
            qm = jnp.where((lane // HEAD_DIM) == e, qp, zero_b)
            cqm = jnp.where((lane // 8) == e, cqp, zero_b)
            lhs = jnp.concatenate([qm, cqm], axis=-1)

            def tile(j, delta, use_tab, carry):
                m, l, acc = carry
                r0 = pl.multiple_of(j * tq, tq)
                rhs = jnp.concatenate([k_ref[pl.ds(r0, tq), cs], ck_ref[pl.ds(r0, tq), cs]], axis=-1)
                z = _dot_nt(lhs, rhs)
                if use_tab:
                    z = z + tab_ref[delta]
                m_new = jnp.maximum(m, jnp.max(z, axis=-1, keepdims=True))
                alpha = jnp.exp(m - m_new)
                pw = jnp.exp(z - m_new)
                l = alpha * l + jnp.sum(pw, axis=-1, keepdims=True)
                acc = alpha * acc + _dot(pw.astype(BF16), v_ref[pl.ds(r0, tq), cs])
                return m_new, l, acc

            carry = (jnp.full((tq, 1), NEG, F32), jnp.zeros((tq, 1), F32), jnp.zeros((tq, LANES), F32))
            carry = tile(i, 0, True, carry)
            if n_tab > 1:
                carry = lax.fori_loop(1, n_prev + 1, lambda d, c: tile(i - d, d, True, c), carry)
            else:
                carry = lax.fori_loop(1, n_prev + 1, lambda d, c: tile(i - d, d, False, c), carry)
            m, l, acc = carry
            outs.append(acc / l)
        o_ref[:, cs] = jnp.where(lane < HEAD_DIM, outs[0], outs[1])


def _attn(main, qcol, cq, ck, table, bsz, seq, cq_per_batch, n_back, tq=256):
    t = main.shape[0]
    nq = seq // tq
    n_tab = table.shape[0]
    if cq_per_batch:
        cq_map = lambda b, i: (b * nq + i, 0)
        ck_map = lambda b, i: (b, 0)
    else:
        cq_map = lambda b, i: (i, 0)
        ck_map = lambda b, i: (0, 0)
    qb = qcol // GROUP_W
    return pl.pallas_call(
        functools.partial(_attn_kernel, tq=tq, n_back=n_back, n_tab=n_tab),
        grid=(bsz, nq),
        in_specs=[
            pl.BlockSpec((tq, GROUP_W), lambda b, i: (b * nq + i, qb)),
            pl.BlockSpec((tq, GROUP_W), cq_map),
            pl.BlockSpec((seq, GROUP_W), lambda b, i: (b, qb + 1)),
            pl.BlockSpec((seq, GROUP_W), ck_map),
            pl.BlockSpec((seq, GROUP_W), lambda b, i: (b, qb + 2)),
            pl.BlockSpec((n_tab, tq, tq), lambda b, i: (0, 0, 0)),
        ],
        out_specs=pl.BlockSpec((tq, GROUP_W), lambda b, i: (b * nq + i, 0)),
        out_shape=jax.ShapeDtypeStruct((t, GROUP_W), F32),
        compiler_params=_cparams(("parallel", "arbitrary")),
        name="attn",
    )(main, cq, main, ck, main, table)


def _fox_table(tq):
    a = np.arange(tq)
    return jnp.asarray(np.where(a[None, :] <= a[:, None], 0.0, NEG)[None].astype(np.float32))


def _dil_consts(seq, tq):
    max_d = max(w for w, _ in DIL_PAIRS)
    n_back = max_d // tq
    a = np.arange(tq)
    tabs = []
    for delta in range(n_back + 1):
        d = delta * tq + a[:, None] - a[None, :]
        mult = np.zeros_like(d)
        for w, r in DIL_PAIRS:
            mult = mult + ((d >= 0) & (d % r == 0) & (d <= w)).astype(d.dtype)
        tabs.append(np.where(mult > 0, np.log(np.maximum(mult, 1)), NEG))
    table = jnp.asarray(np.stack(tabs).astype(np.float32))
    slopes = 2.0 ** (-8.0 * np.arange(1, N_GROUP_HEADS + 1, dtype=np.float32) / N_GROUP_HEADS)
    pos = np.arange(seq, dtype=np.float32)
    cq = jnp.zeros((seq, GROUP_W), F32)
    ck = jnp.zeros((seq, GROUP_W), F32)
    for h in range(N_GROUP_HEADS):
        bq = jnp.asarray(-slopes[h] * pos)
        pieces_q = _split3(bq)
        pieces_k = _split3(-bq)
        for m in range(3):
            cq = cq.at[:, _aug_col(h, m)].set(pieces_q[m].astype(F32))
            ck = ck.at[:, _aug_col(h, m)].set(1.0)
            cq = cq.at[:, _aug_col(h, 3 + m)].set(1.0)
            ck = ck.at[:, _aug_col(h, 3 + m)].set(pieces_k[m].astype(F32))
    return table, cq.astype(BF16), ck.astype(BF16), n_back


def _sb_kernel(q_ref, k_ref, v_ref, u_ref, o_ref, *, tq):
    i = pl.program_id(1)
    lane = lax.broadcasted_iota(jnp.int32, (1, LANES), 1)
    zero_b = jnp.zeros((), BF16)
    u = u_ref[...]
    row = lax.broadcasted_iota(jnp.int32, (tq, tq), 0)
    col = lax.broadcasted_iota(jnp.int32, (tq, tq), 1)
    strict = col < row

    for p in range(2):
        cs = slice(p * LANES, (p + 1) * LANES)
        qp = q_ref[:, cs]
        outs = []
        for e in range(2):
            qm = jnp.where((lane // HEAD_DIM) == e, qp, zero_b)

            def tile(j, diag, carry):
                run, acc = carry
                r0 = pl.multiple_of(j * tq, tq)
                z = _dot_nt(qm, k_ref[pl.ds(r0, tq), cs])
                sp = jnp.log1p(jnp.exp(-jnp.abs(z)))
                lb = jnp.minimum(z, 0.0) - sp
                lom = jnp.minimum(-z, 0.0) - sp
                if diag:
                    lom = jnp.where(strict, lom, 0.0)
                hi = lom.astype(BF16)
                lo = (lom - hi.astype(F32)).astype(BF16)
                tail = _dot(hi, u) + _dot(lo, u)
                w = jnp.exp(lb + tail + run)
                if diag:
                    w = jnp.where(strict, w, 0.0)
                acc = acc + _dot(w.astype(BF16), v_ref[pl.ds(r0, tq), cs])
                run = run + jnp.sum(lom, axis=-1, keepdims=True)
                return run, acc

            carry = (jnp.zeros((tq, 1), F32), jnp.zeros((tq, LANES), F32))
            carry = tile(i, True, carry)
            carry = lax.fori_loop(1, i + 1, lambda d, c: tile(i - d, False, c), carry)
            outs.append(carry[1])
        o_ref[:, cs] = jnp.where(lane < HEAD_DIM, outs[0], outs[1])


def _sb_attn(main, bsz, seq, tq=256):
    t = main.shape[0]
    nq = seq // tq
    a = np.arange(tq)
    u = jnp.asarray((a[:, None] > a[None, :]).astype(np.float32), BF16)
    qb = C_SB // GROUP_W
    return pl.pallas_call(
        functools.partial(_sb_kernel, tq=tq),
        grid=(bsz, nq),
        in_specs=[
            pl.BlockSpec((tq, GROUP_W), lambda b, i: (b * nq + i, qb)),
            pl.BlockSpec((seq, GROUP_W), lambda b, i: (b, qb + 1)),
            pl.BlockSpec((seq, GROUP_W), lambda b, i: (b, qb + 2)),
            pl.BlockSpec((tq, tq), lambda b, i: (0, 0)),
        ],
        out_specs=pl.BlockSpec((tq, GROUP_W), lambda b, i: (b * nq + i, 0)),
        out_shape=jax.ShapeDtypeStruct((t, GROUP_W), F32),
        compiler_params=_cparams(("parallel", "arbitrary")),
        name="sb_attn",
    )(main, main, main, u)


def _gla_kernel(qk_ref, v_ref, aux_ref, wup_ref, b_ref, lcum_ref, lsum_ref, eh_ref, bd_ref,
                o_ref, state_ref, *, tc):
    @pl.when(pl.program_id(1) == 0)
    def _():
        state_ref[...] = jnp.zeros_like(state_ref)

    c = GLA_CHUNK
    n = tc // c
    q = qk_ref[:, 0:LANES].astype(F32)
    k = qk_ref[:, LANES:2 * LANES].astype(F32)
    v = v_ref[...].astype(F32)
    a1, a2, a3 = _split3(aux_ref[...])
    w1 = wup_ref[0]
    w2 = wup_ref[1]
    logit = (_dot(a1, w1) + _dot(a2, w1) + _dot(a3, w1) + _dot(a1, w2) + _dot(a2, w2)
             + _dot(a1, wup_ref[2]) + b_ref[...])
    log_a = jax.nn.log_sigmoid(logit) * (1.0 / GLA_GATE_TAU)
    s1, s2, s3 = _split3(log_a)
    lcum = lcum_ref[...]
    lsum = lsum_ref[...]
    g = _dot(lcum, s1) + _dot(lcum, s2) + _dot(lcum, s3)
    gl = _dot(lsum, s1) + _dot(lsum, s2) + _dot(lsum, s3)
    q_in = q * jnp.exp(g)
    k_out = k * jnp.exp(gl - g)

    eh = eh_ref[...]
    g3 = g.reshape(n, c, LANES)
    k3 = k.reshape(n, c, LANES)
    q3 = q.reshape(n, c, LANES)
    v3 = v.reshape(n, c, GROUP_W)
    tpos = lax.broadcasted_iota(jnp.int32, (n, c, LANES), 1)
    o3 = jnp.zeros((n, c, GROUP_W), F32)
    for s in range(c):
        dec = jnp.exp(jnp.minimum(g3 - g3[:, s:s + 1, :], 0.0))
        pr = jnp.where(tpos >= s, q3 * k3[:, s:s + 1, :] * dec, 0.0)
        sc = _dot(pr.reshape(tc, LANES).astype(BF16), eh)
        o3 = o3 + sc.reshape(n, c, GROUP_W) * v3[:, s:s + 1, :]
    o_intra = o3.reshape(tc, GROUP_W)

    bd = bd_ref[...]
    st = state_ref[...]
    q_in_b = q_in.astype(BF16)
    k_out_b = k_out.astype(BF16)
    v_b = v_ref[...]
    for ci in range(n):
        rs = slice(ci * c, (ci + 1) * c)
        o_ref[rs, :] = o_intra[rs, :] + _dot_nt(q_in_b[rs, :], st.astype(BF16))
        kv = _dot_tn(v_b[rs, :], k_out_b[rs, :]) * bd
        a_c = jnp.exp(gl[ci * c:ci * c + 1, :])
        st = st * a_c + kv
    state_ref[...] = st


def _gla_consts(tc):
    c = GLA_CHUNK
    a = np.arange(tc)
    same = (a[:, None] // c) == (a[None, :] // c)
    lcum = (same & (a[None, :] <= a[:, None])).astype(np.float32)
    lsum = same.astype(np.float32)
    hk = np.arange(LANES) // GLA_KEY_DIM
    hv = np.arange(GROUP_W) // HEAD_DIM
    eh = (hk[:, None] == hv[None, :]).astype(np.float32)
    return (jnp.asarray(lcum, BF16), jnp.asarray(lsum, BF16), jnp.asarray(eh, BF16),
            jnp.asarray(eh.T))


def _gla(main, aux, wup3, b_row, bsz, seq, tc=256):
    t = main.shape[0]
    nt = seq // tc
    lcum, lsum, eh, bd = _gla_consts(tc)
    const2 = lambda b, i: (0, 0)
    return pl.pallas_call(
        functools.partial(_gla_kernel, tc=tc),
        grid=(bsz, nt),
        in_specs=[
            pl.BlockSpec((tc, 2 * LANES), lambda b, i: (b * nt + i, C_GLA_QK // (2 * LANES))),
            pl.BlockSpec((tc, GROUP_W), lambda b, i: (b * nt + i, C_GLA_V // GROUP_W)),
            pl.BlockSpec((tc, LANES), lambda b, i: (b * nt + i, 0)),
            pl.BlockSpec((3, LANES, LANES), lambda b, i: (0, 0, 0)),
            pl.BlockSpec((1, LANES), const2),
            pl.BlockSpec((tc, tc), const2),
            pl.BlockSpec((tc, tc), const2),
            pl.BlockSpec((LANES, GROUP_W), const2),
            pl.BlockSpec((GROUP_W, LANES), const2),
        ],
        out_specs=pl.BlockSpec((tc, GROUP_W), lambda b, i: (b * nt + i, 0)),
        out_shape=jax.ShapeDtypeStruct((t, GROUP_W), F32),
        scratch_shapes=[pltpu.VMEM((GROUP_W, LANES), F32)],
        compiler_params=_cparams(("parallel", "arbitrary")),
        name="gla",
    )(main, main, aux, wup3, b_row, lcum, lsum, eh, bd)


def _mix_out_kernel(x_ref, osb_ref, ogla_ref, odil_ref, ofox_ref, r_ref, g_ref, eavg_ref, w_ref, xo_ref):
    eavg = eavg_ref[...]
    acc = x_ref[...]
    for gi, o_ref in enumerate((osb_ref, ogla_ref, odil_ref, ofox_ref)):
        o = o_ref[...]
        sq = o * o
        hi = sq.astype(BF16)
        lo = (sq - hi.astype(F32)).astype(BF16)
        ms = _dot(hi, eavg) + _dot(lo, eavg)
        y = o * lax.rsqrt(ms + NORM_EPS) * g_ref[:, gi * GROUP_W:(gi + 1) * GROUP_W]
        if gi == 1:
            r = r_ref[...].astype(F32)
            y = y * (r * jax.nn.sigmoid(r))
        acc = acc + _dot(y.astype(BF16), w_ref[gi * GROUP_W:(gi + 1) * GROUP_W, :])
    xo_ref[...] = acc


def _mix_out(x2, o_sb, o_gla, o_dil, o_fox, main, g_row, w_out, tm=512):
    t = x2.shape[0]
    hv = np.arange(GROUP_W) // HEAD_DIM
    eavg = jnp.asarray((hv[:, None] == hv[None, :]).astype(np.float32) / HEAD_DIM, BF16)
    ob = pl.BlockSpec((tm, GROUP_W), lambda i: (i, 0))
    return pl.pallas_call(
        _mix_out_kernel,
        grid=(t // tm,),
        in_specs=[
            pl.BlockSpec((tm, D_MODEL), lambda i: (i, 0)),
            ob, ob, ob, ob,
            pl.BlockSpec((tm, GROUP_W), lambda i: (i, C_GLA_R // GROUP_W)),
            pl.BlockSpec((1, D_MODEL), lambda i: (0, 0)),
            pl.BlockSpec((GROUP_W, GROUP_W), lambda i: (0, 0)),
            pl.BlockSpec((D_MODEL, D_MODEL), lambda i: (0, 0)),
        ],
        out_specs=pl.BlockSpec((tm, D_MODEL), lambda i: (i, 0)),
        out_shape=jax.ShapeDtypeStruct((t, D_MODEL), F32),
        compiler_params=_cparams(("parallel",)),
        name="mix_out",
    )(x2, o_sb, o_gla, o_dil, o_fox, main, g_row, eavg, w_out)


def _mixer_weights(w_in_l, w_gla_up_l, b_gla_l, b_fox_l, head_g_l):
    splits = (GROUP_W,) * 3 + (GLA_QK_W, GLA_QK_W, GROUP_W, GROUP_W, GLA_GATE_RANK) + (GROUP_W,) * 6 + (N_GROUP_HEADS,)
    cols = []
    start = 0
    for w in splits:
        cols.append(w_in_l[:, start:start + w])
        start += w
    (sb_q, sb_k, sb_v, gla_q, gla_k, gla_v, gla_r, gla_gd,
     dil_q, dil_k, dil_v, fox_q, fox_k, fox_v, fox_f) = cols
    hs = HEAD_DIM ** -0.5
    w_main = jnp.concatenate(
        [sb_q * hs, sb_k, sb_v, fox_q * hs, fox_k, fox_v, dil_q * hs, dil_k, dil_v,
         gla_q * (GLA_KEY_DIM ** -0.5), gla_k, gla_v, gla_r], axis=1).astype(BF16)
    w_aux = jnp.concatenate(
        [gla_gd, fox_f, jnp.zeros((D_MODEL, LANES - GLA_GATE_RANK - N_GROUP_HEADS), F32)], axis=1)
    wa_hi = w_aux.astype(BF16)
    wa_lo = (w_aux - wa_hi.astype(F32)).astype(BF16)
    wup = jnp.zeros((LANES, LANES), F32).at[:GLA_GATE_RANK, :].set(w_gla_up_l)
    wup3 = jnp.stack([p for p in _split3(wup)])
    b_gla_row = b_gla_l.reshape(1, LANES)
    b_fox_row = jnp.zeros((1, LANES), F32).at[0, AUX_FOX:AUX_FOX + N_GROUP_HEADS].set(b_fox_l)
    g_row = head_g_l.reshape(1, D_MODEL)
    return w_main, wa_hi, wa_lo, wup3, b_gla_row, b_fox_row, g_row


def _mixer_layer(x2, bsz, seq, norm_g, mw, w_out_bf, consts):
    w_main, wa_hi, wa_lo, wup3, b_gla_row, b_fox_row, g_row = mw
    fox_tab, dil_tab, dil_cq, dil_ck, dil_back = consts
    main, aux = _mix_in(x2, norm_g.reshape(1, D_MODEL), w_main, wa_hi, wa_lo)
    o_sb = _sb_attn(main, bsz, seq)
    o_gla = _gla(main, aux, wup3, b_gla_row, bsz, seq)
    o_dil = _attn(main, C_DIL, dil_cq, dil_ck, dil_tab, bsz, seq, False, dil_back)
    fcq, fck = _fox_prep(aux, b_fox_row, bsz, seq)
    o_fox = _attn(main, C_FOX, fcq, fck, fox_tab, bsz, seq, True, None)
    x2n = _mix_out(x2, o_sb, o_gla, o_dil, o_fox, main, g_row, w_out_bf)
    return x2n, (o_sb, o_gla, o_dil, o_fox)


def _top_rows(x, k):
    rows = []
    for _ in range(k):
        m = jnp.max(x, axis=0, keepdims=True)
        rows.append(m)
        x = jnp.where(x >= m, NEG, x)
    return rows


def _peer_a_kernel(x_ref, g_ref, wq_ref, keys_ref, xn_ref, s1_ref, s2_ref, th_ref):
    xn = _rms(x_ref[...], g_ref[...]).astype(BF16)
    xn_ref[...] = xn
    qt = _dot_nt(wq_ref[...], xn).astype(BF16)
    tm = xn.shape[0]
    pairs = [(p, q) for p in range(PEER_TOPK) for q in range(PEER_TOPK) if (p + 1) * (q + 1) <= PEER_TOPK]
    pad = (-len(pairs)) % 8
    for h in range(PEER_HEADS):
        tops = []
        for c in range(2):
            hc = 2 * h + c
            st = _dot(keys_ref[hc], qt[hc * PEER_HALF:(hc + 1) * PEER_HALF, :])
            (s1_ref if c == 0 else s2_ref)[h] = st
            tops.append(_top_rows(st, PEER_TOPK))
        cand = jnp.concatenate([tops[0][p] + tops[1][q] for p, q in pairs]
                               + [jnp.full((pad, tm), NEG, F32)], axis=0)
        best = _top_rows(cand, PEER_TOPK)
        zsum = jnp.zeros((1, tm), F32)
        for r in range(PEER_TOPK):
            zsum = zsum + jnp.exp(best[r] - best[0])
        th_ref[h:h + 1, :] = best[PEER_TOPK - 1]
        th_ref[PEER_HEADS + h:PEER_HEADS + h + 1, :] = best[0] + jnp.log(zsum)


def _peer_a(x2, g_row, wq_t, keys, tm=512):
    t = x2.shape[0]
    return pl.pallas_call(
        _peer_a_kernel,
        grid=(t // tm,),
        in_specs=[
            pl.BlockSpec((tm, D_MODEL), lambda i: (i, 0)),
            pl.BlockSpec((1, D_MODEL), lambda i: (0, 0)),
            pl.BlockSpec((2 * PEER_HEADS * PEER_HALF, D_MODEL), lambda i: (0, 0)),
            pl.BlockSpec((2 * PEER_HEADS, PEER_N_KEYS, PEER_HALF), lambda i: (0, 0, 0)),
        ],
        out_specs=[
            pl.BlockSpec((tm, D_MODEL), lambda i: (i, 0)),
            pl.BlockSpec((PEER_HEADS, PEER_N_KEYS, tm), lambda i: (0, 0, i)),
            pl.BlockSpec((PEER_HEADS, PEER_N_KEYS, tm), lambda i: (0, 0, i)),
            pl.BlockSpec((2 * PEER_HEADS, tm), lambda i: (0, i)),
        ],
        out_shape=[
            jax.ShapeDtypeStruct((t, D_MODEL), BF16),
            jax.ShapeDtypeStruct((PEER_HEADS, PEER_N_KEYS, t), F32),
            jax.ShapeDtypeStruct((PEER_HEADS, PEER_N_KEYS, t), F32),
            jax.ShapeDtypeStruct((2 * PEER_HEADS, t), F32),
        ],
        compiler_params=_cparams(("parallel",)),
        name="peer_a",
    )(x2, g_row, wq_t, keys)


def _gelu(a):
    return 0.5 * a * (1.0 + lax.erf(a * (2.0 ** -0.5)))


def _peer_b_kernel(xn_ref, u_ref, vt_ref, s1_ref, s2_ref, th_ref, x_ref, o_ref, acc_ref, act_ref, w_ref,
                   s1t_ref, *, te, tm):
    ei = pl.program_id(1)
    n_i = te // PEER_N_KEYS

    @pl.when(ei == 0)
    def _():
        acc_ref[...] = jnp.zeros_like(acc_ref)

    act_ref[...] = _dot_nt(u_ref[...], xn_ref[...])

    i0 = pl.multiple_of(ei * n_i, n_i)
    for h in range(PEER_HEADS):
        blk = s1_ref[h, pl.ds(i0, n_i), :]
        for il in range(n_i):
            s1t_ref[il, h:h + 1, :] = blk[il:il + 1, :]

    def per_i(il, _):
        r0 = pl.multiple_of(il * PEER_N_KEYS, PEER_N_KEYS)
        s1rows = s1t_ref[il]
        for tc in range(tm // LANES):
            ts = slice(tc * LANES, (tc + 1) * LANES)
            gsum = jnp.zeros((PEER_N_KEYS, LANES), F32)
            for h in range(PEER_HEADS):
                sm = s1rows[h:h + 1, ts] + s2_ref[h, :, ts]
                gate = jnp.exp(sm - th_ref[PEER_HEADS + h:PEER_HEADS + h + 1, ts])
                gsum = gsum + jnp.where(sm >= th_ref[h:h + 1, ts], gate, 0.0)
            w_ref[pl.ds(r0, PEER_N_KEYS), ts] = (gsum * _gelu(act_ref[pl.ds(r0, PEER_N_KEYS), ts])).astype(BF16)
        return 0

    lax.fori_loop(0, n_i, per_i, 0)
    acc_ref[...] += _dot(vt_ref[...], w_ref[...])

    @pl.when(ei == pl.num_programs(1) - 1)
    def _():
        o_ref[...] = x_ref[...] + acc_ref[...].T


def _peer_b(x2, xn, u_bf, vt_bf, s1, s2, th, tm=512, te=1024):
    t = x2.shape[0]
    return pl.pallas_call(
        functools.partial(_peer_b_kernel, te=te, tm=tm),
        grid=(t // tm, PEER_N_EXPERTS // te),
        in_specs=[
            pl.BlockSpec((tm, D_MODEL), lambda i, e: (i, 0)),
            pl.BlockSpec((te, D_MODEL), lambda i, e: (e, 0)),
            pl.BlockSpec((D_MODEL, te), lambda i, e: (0, e)),
            pl.BlockSpec((PEER_HEADS, PEER_N_KEYS, tm), lambda i, e: (0, 0, i)),
            pl.BlockSpec((PEER_HEADS, PEER_N_KEYS, tm), lambda i, e: (0, 0, i)),
            pl.BlockSpec((2 * PEER_HEADS, tm), lambda i, e: (0, i)),
            pl.BlockSpec((tm, D_MODEL), lambda i, e: (i, 0)),
        ],
        out_specs=pl.BlockSpec((tm, D_MODEL), lambda i, e: (i, 0)),
        out_shape=jax.ShapeDtypeStruct((t, D_MODEL), F32),
        scratch_shapes=[
            pltpu.VMEM((D_MODEL, tm), F32),
            pltpu.VMEM((te, tm), F32),
            pltpu.VMEM((te, tm), BF16),
            pltpu.VMEM((te // PEER_N_KEYS, PEER_HEADS, tm), F32),
        ],
        compiler_params=_cparams(("parallel", "arbitrary")),
        name="peer_b",
    )(xn, u_bf, vt_bf, s1, s2, th, x2)


def _final_norm_kernel(x_ref, g_ref, o_ref):
    o_ref[...] = _rms(x_ref[...], g_ref[...])


def _final_norm(x2, g_row, tm=1024):
    t = x2.shape[0]
    return pl.pallas_call(
        _final_norm_kernel,
        grid=(t // tm,),
        in_specs=[pl.BlockSpec((tm, D_MODEL), lambda i: (i, 0)), pl.BlockSpec((1, D_MODEL), lambda i: (0, 0))],
        out_specs=pl.BlockSpec((tm, D_MODEL), lambda i: (i, 0)),
        out_shape=jax.ShapeDtypeStruct((t, D_MODEL), F32),
        compiler_params=_cparams(("parallel",)),
        name="final_norm",
    )(x2, g_row)


def _peer_layer(x2, norm_g, w_peer_q_l, sub_keys_l, peer_u_l, peer_v_l):
    wq_t = w_peer_q_l.T.astype(BF16)
    keys = sub_keys_l.reshape(2 * PEER_HEADS, PEER_N_KEYS, PEER_HALF).astype(BF16)
    xn, s1, s2, th = _peer_a(x2, norm_g.reshape(1, D_MODEL), wq_t, keys)
    return _peer_b(x2, xn, peer_u_l.astype(BF16), peer_v_l.T.astype(BF16), s1, s2, th)


def kernel(x, norm_mix_g, w_in, w_gla_up, b_gla, b_fox, head_norm_g, w_out, norm_ffn_g,
           w_peer_q, peer_sub_keys, peer_u, peer_v, norm_final_g):
    bsz, seq, _ = x.shape
    depth = w_in.shape[0]
    tq = 256
    dil_tab, dil_cq, dil_ck, dil_back = _dil_consts(seq, tq)
    consts = (_fox_table(tq), dil_tab, dil_cq, dil_ck, dil_back)
    x2 = x.reshape(bsz * seq, D_MODEL)
    for l in range(depth):
        mw = _mixer_weights(w_in[l], w_gla_up[l], b_gla[l], b_fox[l], head_norm_g[l])
        x2, _ = _mixer_layer(x2, bsz, seq, norm_mix_g[l], mw, w_out[l].astype(BF16), consts)
        x2 = _peer_layer(x2, norm_ffn_g[l], w_peer_q[l], peer_sub_keys[l], peer_u[l], peer_v[l])
    return _final_norm(x2, norm_final_g.reshape(1, D_MODEL)).reshape(bsz, seq, D_MODEL)
```

```python
import functools
import math

import numpy as np
import jax
import jax.numpy as jnp
from jax import lax
from jax.experimental import pallas as pl
from jax.experimental.pallas import tpu as pltpu

F32 = jnp.float32
BF16 = jnp.bfloat16

D_MODEL = 1024
HEAD_DIM = 64
N_GROUP_HEADS = 4
GROUP_W = N_GROUP_HEADS * HEAD_DIM
GLA_KEY_DIM = 32
GLA_QK_W = N_GROUP_HEADS * GLA_KEY_DIM
GLA_GATE_RANK = 16
GLA_GATE_TAU = 16.0
GLA_CHUNK = 16
DIL_PAIRS = ((128, 1), (512, 4), (2048, 16))
PEER_HEADS = 8
PEER_N_KEYS = 128
PEER_N_EXPERTS = PEER_N_KEYS * PEER_N_KEYS
PEER_HALF = 128
PEER_TOPK = 16
NORM_EPS = 1e-6

LANES = 128
MXU_DIM = 256
VMEM_LIMIT = 56 * 1024 * 1024

NEG = -1e30

C_SB = 0
C_FOX = 768
C_DIL = 1536
C_GLA_QK = 2304
C_GLA_V = 2560
C_GLA_R = 2816
MAIN_W = 3072
AUX_GD = 0
AUX_FOX = 16


def _cparams(sem):
    return pltpu.CompilerParams(dimension_semantics=sem, vmem_limit_bytes=VMEM_LIMIT)


def _split3(x):
    a = x.astype(BF16)
    r = x - a.astype(F32)
    b = r.astype(BF16)
    c = (r - b.astype(F32)).astype(BF16)
    return a, b, c


def _dot(a, b):
    return jnp.dot(a, b, preferred_element_type=F32)


def _dot_nt(a, b):
    return lax.dot_general(a, b, (((1,), (1,)), ((), ())), preferred_element_type=F32)


def _dot_tn(a, b):
    return lax.dot_general(a, b, (((0,), (0,)), ((), ())), preferred_element_type=F32)


def _rms(xf, g):
    ms = jnp.mean(xf * xf, axis=-1, keepdims=True)
    return xf * lax.rsqrt(ms + NORM_EPS) * g


def _mix_in_kernel(x_ref, g_ref, w_ref, wah_ref, wal_ref, main_ref, aux_ref, *, n_chunk):
    y = _rms(x_ref[...], g_ref[...])
    y_hi = y.astype(BF16)
    y_lo = (y - y_hi.astype(F32)).astype(BF16)
    cw = MAIN_W // n_chunk
    for c in range(n_chunk):
        main_ref[:, c * cw:(c + 1) * cw] = _dot(y_hi, w_ref[:, c * cw:(c + 1) * cw]).astype(BF16)
    wah = wah_ref[...]
    aux_ref[...] = _dot(y_hi, wah) + _dot(y_hi, wal_ref[...]) + _dot(y_lo, wah)


def _mix_in(x2, g, w_main, wa_hi, wa_lo, tm=512):
    t = x2.shape[0]
    return pl.pallas_call(
        functools.partial(_mix_in_kernel, n_chunk=6),
        grid=(t // tm,),
        in_specs=[
            pl.BlockSpec((tm, D_MODEL), lambda i: (i, 0)),
            pl.BlockSpec((1, D_MODEL), lambda i: (0, 0)),
            pl.BlockSpec((D_MODEL, MAIN_W), lambda i: (0, 0)),
            pl.BlockSpec((D_MODEL, LANES), lambda i: (0, 0)),
            pl.BlockSpec((D_MODEL, LANES), lambda i: (0, 0)),
        ],
        out_specs=[
            pl.BlockSpec((tm, MAIN_W), lambda i: (i, 0)),
            pl.BlockSpec((tm, LANES), lambda i: (i, 0)),
        ],
        out_shape=[jax.ShapeDtypeStruct((t, MAIN_W), BF16), jax.ShapeDtypeStruct((t, LANES), F32)],
        compiler_params=_cparams(("parallel",)),
        name="mix_in",
    )(x2, g, w_main, wa_hi, wa_lo)


def _fox_prep_kernel(aux_ref, b_ref, ltri_ref, pq_ref, pk_ref, eq_ref, ek_ref, cq_ref, ck_ref, carry_ref):
    @pl.when(pl.program_id(1) == 0)
    def _():
        carry_ref[...] = jnp.zeros_like(carry_ref)

    lf = jax.nn.log_sigmoid(aux_ref[...] + b_ref[...])
    l1, l2, l3 = _split3(lf)
    ltri = ltri_ref[...]
    cf = _dot(ltri, l1) + _dot(ltri, l2) + _dot(ltri, l3) + carry_ref[0:1, :]
    tp = cf.shape[0]
    carry_ref[...] = jnp.broadcast_to(cf[tp - 1:tp, :], carry_ref.shape)
    c1, c2, c3 = _split3(cf)
    cq = _dot(c1, pq_ref[0]) + _dot(c2, pq_ref[1]) + _dot(c3, pq_ref[2]) + eq_ref[...]
    ck = _dot(c1, pk_ref[0]) + _dot(c2, pk_ref[1]) + _dot(c3, pk_ref[2]) + ek_ref[...]
    cq_ref[...] = cq.astype(BF16)
    ck_ref[...] = ck.astype(BF16)


def _aug_col(h, m):
    return (h // 2) * LANES + (h % 2) * 8 + m


def _fox_prep_consts():
    pq = np.zeros((3, LANES, GROUP_W), np.float32)
    pk = np.zeros((3, LANES, GROUP_W), np.float32)
    eq = np.zeros((1, GROUP_W), np.float32)
    ek = np.zeros((1, GROUP_W), np.float32)
    for h in range(N_GROUP_HEADS):
        for m in range(3):
            pq[m, AUX_FOX + h, _aug_col(h, m)] = 1.0
            ek[0, _aug_col(h, m)] = 1.0
            eq[0, _aug_col(h, 3 + m)] = 1.0
            pk[m, AUX_FOX + h, _aug_col(h, 3 + m)] = -1.0
    return jnp.asarray(pq, BF16), jnp.asarray(pk, BF16), jnp.asarray(eq), jnp.asarray(ek)


def _fox_prep(aux, b_row, bsz, seq, tp=256):
    t = aux.shape[0]
    nt = seq // tp
    ltri = jnp.asarray(np.tril(np.ones((tp, tp), np.float32)), BF16)
    pq, pk, eq, ek = _fox_prep_consts()
    const2 = lambda b, i: (0, 0)
    return pl.pallas_call(
        _fox_prep_kernel,
        grid=(bsz, nt),
        in_specs=[
            pl.BlockSpec((tp, LANES), lambda b, i: (b * nt + i, 0)),
            pl.BlockSpec((1, LANES), const2),
            pl.BlockSpec((tp, tp), const2),
            pl.BlockSpec((3, LANES, GROUP_W), lambda b, i: (0, 0, 0)),
            pl.BlockSpec((3, LANES, GROUP_W), lambda b, i: (0, 0, 0)),
            pl.BlockSpec((1, GROUP_W), const2),
            pl.BlockSpec((1, GROUP_W), const2),
        ],
        out_specs=[
            pl.BlockSpec((tp, GROUP_W), lambda b, i: (b * nt + i, 0)),
            pl.BlockSpec((tp, GROUP_W), lambda b, i: (b * nt + i, 0)),
        ],
        out_shape=[jax.ShapeDtypeStruct((t, GROUP_W), BF16), jax.ShapeDtypeStruct((t, GROUP_W), BF16)],
        scratch_shapes=[pltpu.VMEM((8, LANES), F32)],
        compiler_params=_cparams(("arbitrary", "arbitrary")),
        name="fox_prep",
    )(aux, b_row, ltri, pq, pk, eq, ek)


def _attn_kernel(q_ref, cq_ref, k_ref, ck_ref, v_ref, tab_ref, o_ref, acc_ref, *, tq, n_back, n_tab):
    i = pl.program_id(1)
    lane = lax.broadcasted_iota(jnp.int32, (1, LANES), 1)
    n_prev = i if n_back is None else jnp.minimum(i, n_back)
    zero_b = jnp.zeros((), BF16)
    one_b = jnp.ones((), BF16)
    heads = range(N_GROUP_HEADS)

    lhs = []
    for h in heads:
        cs = slice((h // 2) * LANES, (h // 2 + 1) * LANES)
        qm = jnp.where((lane // HEAD_DIM) == h % 2, q_ref[:, cs], zero_b)
        cqm = jnp.where((lane // 8) == h % 2, cq_ref[:, cs], zero_b)
        lhs.append(jnp.concatenate([qm, cqm], axis=-1))

    def tile(j, delta, use_tab, ms):
        r0 = pl.multiple_of(j * tq, tq)
        new_ms = []
        for h in heads:
            cs = slice((h // 2) * LANES, (h // 2 + 1) * LANES)
            rhs = jnp.concatenate([k_ref[pl.ds(r0, tq), cs], ck_ref[pl.ds(r0, tq), cs]], axis=-1)
            z = _dot_nt(lhs[h], rhs)
            if use_tab:
                z = z + tab_ref[delta]
            m_new = jnp.maximum(ms[h], jnp.max(z, axis=-1, keepdims=True))
            alpha = jnp.exp(ms[h] - m_new)
            pw = jnp.exp(z - m_new).astype(BF16)
            v_aug = jnp.where((lane // HEAD_DIM) == h % 2, v_ref[pl.ds(r0, tq), cs], one_b)
            acc_ref[h] = alpha * acc_ref[h] + _dot(pw, v_aug)
            new_ms.append(m_new)
        return tuple(new_ms)

    acc_ref[...] = jnp.zeros_like(acc_ref)
    ms = tuple(jnp.full((tq, 1), NEG, F32) for _ in heads)
    ms = tile(i, 0, True, ms)
    lax.fori_loop(1, n_prev + 1, lambda d, c: tile(i - d, d, n_tab > 1, c), ms)

    for p in range(2):
        a0 = acc_ref[2 * p]
        a1 = acc_ref[2 * p + 1]
        num = jnp.where(lane < HEAD_DIM, a0, a1)
        den = jnp.where(lane < HEAD_DIM, pltpu.roll(a0, HEAD_DIM, 1), pltpu.roll(a1, HEAD_DIM, 1))
        o_ref[:, p * LANES:(p + 1) * LANES] = num / den


def _attn(main, qcol, cq, ck, table, bsz, seq, cq_per_batch, n_back, tq=256):
    t = main.shape[0]
    nq = seq // tq
    n_tab = table.shape[0]
    if cq_per_batch:
        cq_map = lambda b, i: (b * nq + i, 0)
        ck_map = lambda b, i: (b, 0)
    else:
        cq_map = lambda b, i: (i, 0)
        ck_map = lambda b, i: (0, 0)
    qb = qcol // GROUP_W
    return pl.pallas_call(
        functools.partial(_attn_kernel, tq=tq, n_back=n_back, n_tab=n_tab),
        grid=(bsz, nq),
        in_specs=[
            pl.BlockSpec((tq, GROUP_W), lambda b, i: (b * nq + i, qb)),
            pl.BlockSpec((tq, GROUP_W), cq_map),
            pl.BlockSpec((seq, GROUP_W), lambda b, i: (b, qb + 1)),
            pl.BlockSpec((seq, GROUP_W), ck_map),
            pl.BlockSpec((seq, GROUP_W), lambda b, i: (b, qb + 2)),
            pl.BlockSpec((n_tab, tq, tq), lambda b, i: (0, 0, 0)),
        ],
        out_specs=pl.BlockSpec((tq, GROUP_W), lambda b, i: (b * nq + i, 0)),
        out_shape=jax.ShapeDtypeStruct((t, GROUP_W), F32),
        scratch_shapes=[pltpu.VMEM((N_GROUP_HEADS, tq, LANES), F32)],
        compiler_params=_cparams(("parallel", "arbitrary")),
        name="attn",
    )(main, cq, main, ck, main, table)


def _fox_table(tq):
    a = np.arange(tq)
    return jnp.asarray(np.where(a[None, :] <= a[:, None], 0.0, NEG)[None].astype(np.float32))


def _dil_consts(seq, tq):
    max_d = max(w for w, _ in DIL_PAIRS)
    n_back = max_d // tq
    a = np.arange(tq)
    tabs = []
    for delta in range(n_back + 1):
        d = delta * tq + a[:, None] - a[None, :]
        mult = np.zeros_like(d)
        for w, r in DIL_PAIRS:
            mult = mult + ((d >= 0) & (d % r == 0) & (d <= w)).astype(d.dtype)
        tabs.append(np.where(mult > 0, np.log(np.maximum(mult, 1)), NEG))
    table = jnp.asarray(np.stack(tabs).astype(np.float32))
    slopes = 2.0 ** (-8.0 * np.arange(1, N_GROUP_HEADS + 1, dtype=np.float32) / N_GROUP_HEADS)
    pos = np.arange(seq, dtype=np.float32)
    cq = jnp.zeros((seq, GROUP_W), F32)
    ck = jnp.zeros((seq, GROUP_W), F32)
    for h in range(N_GROUP_HEADS):
        bq = jnp.asarray(-slopes[h] * pos)
        pieces_q = _split3(bq)
        pieces_k = _split3(-bq)
        for m in range(3):
            cq = cq.at[:, _aug_col(h, m)].set(pieces_q[m].astype(F32))
            ck = ck.at[:, _aug_col(h, m)].set(1.0)
            cq = cq.at[:, _aug_col(h, 3 + m)].set(1.0)
            ck = ck.at[:, _aug_col(h, 3 + m)].set(pieces_k[m].astype(F32))
    return table, cq.astype(BF16), ck.astype(BF16), n_back


def _sb_kernel(q_ref, k_ref, v_ref, u_ref, o_ref, acc_ref, run_ref, *, tq):
    i = pl.program_id(1)
    lane = lax.broadcasted_iota(jnp.int32, (1, LANES), 1)
    zero_b = jnp.zeros((), BF16)
    row = lax.broadcasted_iota(jnp.int32, (tq, tq), 0)
    col = lax.broadcasted_iota(jnp.int32, (tq, tq), 1)
    strict = col < row
    heads = range(N_GROUP_HEADS)
    qm = [jnp.where((lane // HEAD_DIM) == h % 2, q_ref[:, (h // 2) * LANES:(h // 2 + 1) * LANES], zero_b)
          for h in heads]

    def tile(j, diag):
        r0 = pl.multiple_of(j * tq, tq)
        for h in heads:
            cs = slice((h // 2) * LANES, (h // 2 + 1) * LANES)
            z = _dot_nt(qm[h], k_ref[pl.ds(r0, tq), cs])
            sp = jnp.log(1.0 + jnp.exp(-jnp.abs(z)))
            lb = jnp.minimum(z, 0.0) - sp
            lom = lb - z
            if diag:
                lom = jnp.where(strict, lom, 0.0)
            hi = lom.astype(BF16)
            lo = (lom - hi.astype(F32)).astype(BF16)
            r = _dot(jnp.concatenate([hi, lo], axis=0), u_ref[...])
            ts = r[:tq] + r[tq:]
            run = run_ref[h]
            w = jnp.exp(lb + ts[:, :tq] + jnp.concatenate([run] * (tq // LANES), axis=-1))
            if diag:
                w = jnp.where(strict, w, 0.0)
            acc_ref[h] += _dot(w.astype(BF16), v_ref[pl.ds(r0, tq), cs])
            run_ref[h] = run + ts[:, tq:]
        return 0

    acc_ref[...] = jnp.zeros_like(acc_ref)
    run_ref[...] = jnp.zeros_like(run_ref)
    tile(i, True)
    lax.fori_loop(1, i + 1, lambda d, c: tile(i - d, False), 0)
    for p in range(2):
        o_ref[:, p * LANES:(p + 1) * LANES] = jnp.where(lane < HEAD_DIM, acc_ref[2 * p], acc_ref[2 * p + 1])


def _sb_attn(main, bsz, seq, tq=256):
    t = main.shape[0]
    nq = seq // tq
    a = np.arange(tq)
    u = jnp.asarray(np.concatenate([(a[:, None] > a[None, :]).astype(np.float32),
                                    np.ones((tq, LANES), np.float32)], axis=1), BF16)
    qb = C_SB // GROUP_W
    return pl.pallas_call(
        functools.partial(_sb_kernel, tq=tq),
        grid=(bsz, nq),
        in_specs=[
            pl.BlockSpec((tq, GROUP_W), lambda b, i: (b * nq + i, qb)),
            pl.BlockSpec((seq, GROUP_W), lambda b, i: (b, qb + 1)),
            pl.BlockSpec((seq, GROUP_W), lambda b, i: (b, qb + 2)),
            pl.BlockSpec((tq, tq + LANES), lambda b, i: (0, 0)),
        ],
        out_specs=pl.BlockSpec((tq, GROUP_W), lambda b, i: (b * nq + i, 0)),
        out_shape=jax.ShapeDtypeStruct((t, GROUP_W), F32),
        scratch_shapes=[pltpu.VMEM((N_GROUP_HEADS, tq, LANES), F32),
                        pltpu.VMEM((N_GROUP_HEADS, tq, LANES), F32)],
        compiler_params=_cparams(("parallel", "arbitrary")),
        name="sb_attn",
    )(main, main, main, u)


def _gla_kernel(qk_ref, v_ref, aux_ref, wup_ref, b_ref, lcum_ref, lsum_ref, eh_ref, bd_ref,
                o_ref, state_ref, *, tc):
    @pl.when(pl.program_id(1) == 0)
    def _():
        state_ref[...] = jnp.zeros_like(state_ref)

    c = GLA_CHUNK
    n = tc // c
    q = qk_ref[:, 0:LANES].astype(F32)
    k = qk_ref[:, LANES:2 * LANES].astype(F32)
    v = v_ref[...].astype(F32)
    a1, a2, a3 = _split3(aux_ref[...])
    w1 = wup_ref[0]
    w2 = wup_ref[1]
    logit = (_dot(a1, w1) + _dot(a2, w1) + _dot(a3, w1) + _dot(a1, w2) + _dot(a2, w2)
             + _dot(a1, wup_ref[2]) + b_ref[...])
    log_a = jax.nn.log_sigmoid(logit) * (1.0 / GLA_GATE_TAU)
    s1, s2, s3 = _split3(log_a)
    lcum = lcum_ref[...]
    lsum = lsum_ref[...]
    g = _dot(lcum, s1) + _dot(lcum, s2) + _dot(lcum, s3)
    gl = _dot(lsum, s1) + _dot(lsum, s2) + _dot(lsum, s3)
    q_in = q * jnp.exp(g)
    k_out = k * jnp.exp(gl - g)

    eh = eh_ref[...]
    g3 = g.reshape(n, c, LANES)
    k3 = k.reshape(n, c, LANES)
    q3 = q.reshape(n, c, LANES)
    v3 = v.reshape(n, c, GROUP_W)
    tpos = lax.broadcasted_iota(jnp.int32, (n, c, LANES), 1)
    o3 = jnp.zeros((n, c, GROUP_W), F32)
    for s in range(c):
        dec = jnp.exp(jnp.minimum(g3 - g3[:, s:s + 1, :], 0.0))
        pr = jnp.where(tpos >= s, q3 * k3[:, s:s + 1, :] * dec, 0.0)
        sc = _dot(pr.reshape(tc, LANES).astype(BF16), eh)
        o3 = o3 + sc.reshape(n, c, GROUP_W) * v3[:, s:s + 1, :]
    o_intra = o3.reshape(tc, GROUP_W)

    bd = bd_ref[...]
    st = state_ref[...]
    q_in_b = q_in.astype(BF16)
    k_out_b = k_out.astype(BF16)
    v_b = v_ref[...]
    for ci in range(n):
        rs = slice(ci * c, (ci + 1) * c)
        o_ref[rs, :] = o_intra[rs, :] + _dot_nt(q_in_b[rs, :], st.astype(BF16))
        kv = _dot_tn(v_b[rs, :], k_out_b[rs, :]) * bd
        a_c = jnp.exp(gl[ci * c:ci * c + 1, :])
        st = st * a_c + kv
    state_ref[...] = st


def _gla_consts(tc):
    c = GLA_CHUNK
    a = np.arange(tc)
    same = (a[:, None] // c) == (a[None, :] // c)
    lcum = (same & (a[None, :] <= a[:, None])).astype(np.float32)
    lsum = same.astype(np.float32)
    hk = np.arange(LANES) // GLA_KEY_DIM
    hv = np.arange(GROUP_W) // HEAD_DIM
    eh = (hk[:, None] == hv[None, :]).astype(np.float32)
    return (jnp.asarray(lcum, BF16), jnp.asarray(lsum, BF16), jnp.asarray(eh, BF16),
            jnp.asarray(eh.T))


def _gla(main, aux, wup3, b_row, bsz, seq, tc=256):
    t = main.shape[0]
    nt = seq // tc
    lcum, lsum, eh, bd = _gla_consts(tc)
    const2 = lambda b, i: (0, 0)
    return pl.pallas_call(
        functools.partial(_gla_kernel, tc=tc),
        grid=(bsz, nt),
        in_specs=[
            pl.BlockSpec((tc, 2 * LANES), lambda b, i: (b * nt + i, C_GLA_QK // (2 * LANES))),
            pl.BlockSpec((tc, GROUP_W), lambda b, i: (b * nt + i, C_GLA_V // GROUP_W)),
            pl.BlockSpec((tc, LANES), lambda b, i: (b * nt + i, 0)),
            pl.BlockSpec((3, LANES, LANES), lambda b, i: (0, 0, 0)),
            pl.BlockSpec((1, LANES), const2),
            pl.BlockSpec((tc, tc), const2),
            pl.BlockSpec((tc, tc), const2),
            pl.BlockSpec((LANES, GROUP_W), const2),
            pl.BlockSpec((GROUP_W, LANES), const2),
        ],
        out_specs=pl.BlockSpec((tc, GROUP_W), lambda b, i: (b * nt + i, 0)),
        out_shape=jax.ShapeDtypeStruct((t, GROUP_W), F32),
        scratch_shapes=[pltpu.VMEM((GROUP_W, LANES), F32)],
        compiler_params=_cparams(("parallel", "arbitrary")),
        name="gla",
    )(main, main, aux, wup3, b_row, lcum, lsum, eh, bd)


def _mix_out_kernel(x_ref, osb_ref, ogla_ref, odil_ref, ofox_ref, r_ref, g_ref, eavg_ref, w_ref, xo_ref):
    eavg = eavg_ref[...]
    acc = x_ref[...]
    for gi, o_ref in enumerate((osb_ref, ogla_ref, odil_ref, ofox_ref)):
        o = o_ref[...]
        sq = o * o
        hi = sq.astype(BF16)
        lo = (sq - hi.astype(F32)).astype(BF16)
        ms = _dot(hi, eavg) + _dot(lo, eavg)
        y = o * lax.rsqrt(ms + NORM_EPS) * g_ref[:, gi * GROUP_W:(gi + 1) * GROUP_W]
        if gi == 1:
            r = r_ref[...].astype(F32)
            y = y * (r * jax.nn.sigmoid(r))
        acc = acc + _dot(y.astype(BF16), w_ref[gi * GROUP_W:(gi + 1) * GROUP_W, :])
    xo_ref[...] = acc


def _mix_out(x2, o_sb, o_gla, o_dil, o_fox, main, g_row, w_out, tm=512):
    t = x2.shape[0]
    hv = np.arange(GROUP_W) // HEAD_DIM
    eavg = jnp.asarray((hv[:, None] == hv[None, :]).astype(np.float32) / HEAD_DIM, BF16)
    ob = pl.BlockSpec((tm, GROUP_W), lambda i: (i, 0))
    return pl.pallas_call(
        _mix_out_kernel,
        grid=(t // tm,),
        in_specs=[
            pl.BlockSpec((tm, D_MODEL), lambda i: (i, 0)),
            ob, ob, ob, ob,
            pl.BlockSpec((tm, GROUP_W), lambda i: (i, C_GLA_R // GROUP_W)),
            pl.BlockSpec((1, D_MODEL), lambda i: (0, 0)),
            pl.BlockSpec((GROUP_W, GROUP_W), lambda i: (0, 0)),
            pl.BlockSpec((D_MODEL, D_MODEL), lambda i: (0, 0)),
        ],
        out_specs=pl.BlockSpec((tm, D_MODEL), lambda i: (i, 0)),
        out_shape=jax.ShapeDtypeStruct((t, D_MODEL), F32),
        compiler_params=_cparams(("parallel",)),
        name="mix_out",
    )(x2, o_sb, o_gla, o_dil, o_fox, main, g_row, eavg, w_out)


def _mixer_weights(w_in_l, w_gla_up_l, b_gla_l, b_fox_l, head_g_l):
    splits = (GROUP_W,) * 3 + (GLA_QK_W, GLA_QK_W, GROUP_W, GROUP_W, GLA_GATE_RANK) + (GROUP_W,) * 6 + (N_GROUP_HEADS,)
    cols = []
    start = 0
    for w in splits:
        cols.append(w_in_l[:, start:start + w])
        start += w
    (sb_q, sb_k, sb_v, gla_q, gla_k, gla_v, gla_r, gla_gd,
     dil_q, dil_k, dil_v, fox_q, fox_k, fox_v, fox_f) = cols
    hs = HEAD_DIM ** -0.5
    w_main = jnp.concatenate(
        [sb_q * hs, sb_k, sb_v, fox_q * hs, fox_k, fox_v, dil_q * hs, dil_k, dil_v,
         gla_q * (GLA_KEY_DIM ** -0.5), gla_k, gla_v, gla_r], axis=1).astype(BF16)
    w_aux = jnp.concatenate(
        [gla_gd, fox_f, jnp.zeros((D_MODEL, LANES - GLA_GATE_RANK - N_GROUP_HEADS), F32)], axis=1)
    wa_hi = w_aux.astype(BF16)
    wa_lo = (w_aux - wa_hi.astype(F32)).astype(BF16)
    wup = jnp.zeros((LANES, LANES), F32).at[:GLA_GATE_RANK, :].set(w_gla_up_l)
    wup3 = jnp.stack([p for p in _split3(wup)])
    b_gla_row = b_gla_l.reshape(1, LANES)
    b_fox_row = jnp.zeros((1, LANES), F32).at[0, AUX_FOX:AUX_FOX + N_GROUP_HEADS].set(b_fox_l)
    g_row = head_g_l.reshape(1, D_MODEL)
    return w_main, wa_hi, wa_lo, wup3, b_gla_row, b_fox_row, g_row


def _mixer_layer(x2, bsz, seq, norm_g, mw, w_out_bf, consts):
    w_main, wa_hi, wa_lo, wup3, b_gla_row, b_fox_row, g_row = mw
    fox_tab, dil_tab, dil_cq, dil_ck, dil_back = consts
    main, aux = _mix_in(x2, norm_g.reshape(1, D_MODEL), w_main, wa_hi, wa_lo)
    o_sb = _sb_attn(main, bsz, seq)
    o_gla = _gla(main, aux, wup3, b_gla_row, bsz, seq)
    o_dil = _attn(main, C_DIL, dil_cq, dil_ck, dil_tab, bsz, seq, False, dil_back)
    fcq, fck = _fox_prep(aux, b_fox_row, bsz, seq)
    o_fox = _attn(main, C_FOX, fcq, fck, fox_tab, bsz, seq, True, None)
    x2n = _mix_out(x2, o_sb, o_gla, o_dil, o_fox, main, g_row, w_out_bf)
    return x2n, (o_sb, o_gla, o_dil, o_fox)


def _top_rows(x, k):
    rows = []
    for _ in range(k):
        m = jnp.max(x, axis=0, keepdims=True)
        rows.append(m)
        x = jnp.where(x >= m, NEG, x)
    return rows


def _top_rows_rank(x, k):
    rows = []
    rank = jnp.full(x.shape, float(k), F32)
    for r in range(k):
        m = jnp.max(x, axis=0, keepdims=True)
        rows.append(m)
        hit = x >= m
        rank = jnp.where(hit, float(r), rank)
        x = jnp.where(hit, NEG, x)
    return rows, rank


_PEER_PAIRS = [(p, q) for p in range(PEER_TOPK) for q in range(PEER_TOPK) if (p + 1) * (q + 1) <= PEER_TOPK]
_PEER_CAND_ROWS = -(-len(_PEER_PAIRS) // 16) * 16


def _peer_a_kernel(x_ref, g_ref, wq_ref, keys_ref, seg_ref, xnt_ref, rank_ref, gb_ref, cnt_ref, ga_ref):
    xnt = _rms(x_ref[...], g_ref[...]).T.astype(BF16)
    xnt_ref[...] = xnt
    qt = _dot(wq_ref[...], xnt).astype(BF16)
    tm = xnt.shape[1]
    pad = _PEER_CAND_ROWS - len(_PEER_PAIRS)
    for h in range(PEER_HEADS):
        st1 = _dot(keys_ref[2 * h], qt[(2 * h) * PEER_HALF:(2 * h + 1) * PEER_HALF, :])
        st2 = _dot(keys_ref[2 * h + 1], qt[(2 * h + 1) * PEER_HALF:(2 * h + 2) * PEER_HALF, :])
        a_rows = _top_rows(st1, PEER_TOPK)
        b_rows, rank2 = _top_rows_rank(st2, PEER_TOPK)
        cand = jnp.concatenate([a_rows[p] + b_rows[q] for p, q in _PEER_PAIRS]
                               + [jnp.full((pad, tm), NEG, F32)], axis=0)
        best = _top_rows(cand, PEER_TOPK)
        zsum = jnp.zeros((1, tm), F32)
        for r in range(PEER_TOPK):
            zsum = zsum + jnp.exp(best[r] - best[0])
        sel = jnp.where(cand >= best[PEER_TOPK - 1], 1.0, 0.0).astype(BF16)
        cnt_rows = _dot(seg_ref[...], sel)
        cnt = jnp.zeros_like(st1)
        for p in range(PEER_TOPK):
            cnt = jnp.where(st1 == a_rows[p], cnt_rows[p:p + 1, :], cnt)
        rank_ref[h] = rank2.astype(BF16)
        gb_ref[h] = jnp.exp(st2 - b_rows[0]).astype(BF16)
        cnt_ref[h] = cnt
        ga_ref[h] = jnp.exp(st1 - (a_rows[0] + jnp.log(zsum)))


def _peer_a(x2, g_row, wq_t, keys, tm=512):
    t = x2.shape[0]
    seg = np.zeros((PEER_TOPK, _PEER_CAND_ROWS), np.float32)
    for r, (p, _) in enumerate(_PEER_PAIRS):
        seg[p, r] = 1.0
    sblk = pl.BlockSpec((PEER_HEADS, PEER_N_KEYS, tm), lambda i: (0, 0, i))
    return pl.pallas_call(
        _peer_a_kernel,
        grid=(t // tm,),
        in_specs=[
            pl.BlockSpec((tm, D_MODEL), lambda i: (i, 0)),
            pl.BlockSpec((1, D_MODEL), lambda i: (0, 0)),
            pl.BlockSpec((2 * PEER_HEADS * PEER_HALF, D_MODEL), lambda i: (0, 0)),
            pl.BlockSpec((2 * PEER_HEADS, PEER_N_KEYS, PEER_HALF), lambda i: (0, 0, 0)),
            pl.BlockSpec((PEER_TOPK, _PEER_CAND_ROWS), lambda i: (0, 0)),
        ],
        out_specs=[pl.BlockSpec((D_MODEL, tm), lambda i: (0, i)), sblk, sblk, sblk, sblk],
        out_shape=[
            jax.ShapeDtypeStruct((D_MODEL, t), BF16),
            jax.ShapeDtypeStruct((PEER_HEADS, PEER_N_KEYS, t), BF16),
            jax.ShapeDtypeStruct((PEER_HEADS, PEER_N_KEYS, t), BF16),
            jax.ShapeDtypeStruct((PEER_HEADS, PEER_N_KEYS, t), F32),
            jax.ShapeDtypeStruct((PEER_HEADS, PEER_N_KEYS, t), F32),
        ],
        compiler_params=_cparams(("parallel",)),
        name="peer_a",
    )(x2, g_row, wq_t, keys, jnp.asarray(seg, BF16))


def _gelu(a):
    return 0.5 * a * (1.0 + lax.erf(a * (2.0 ** -0.5)))


BF16_ROWS = 8
BF16_LANES = 2 * LANES


def _peer_b_kernel(xnt_ref, u_ref, vt_ref, rank_ref, gb_ref, cnt_ref, ga_ref, x_ref, o_ref,
                   acc_ref, act_ref, w_ref, cb_ref, ab_ref, *, te, tm, n_sub):
    ei = pl.program_id(1)
    n_i = te // PEER_N_KEYS
    zero_b = jnp.zeros((), BF16)

    @pl.when(ei == 0)
    def _():
        acc_ref[...] = jnp.zeros_like(acc_ref)

    xnt = xnt_ref[...]
    for s in range(n_sub):
        act_ref[s] = _dot(u_ref[s * te:(s + 1) * te, :], xnt)

    for s in range(n_sub):
        i0 = pl.multiple_of((ei * n_sub + s) * n_i, n_i)
        for h in range(PEER_HEADS):
            cblk = cnt_ref[h, pl.ds(i0, n_i), :]
            ablk = ga_ref[h, pl.ds(i0, n_i), :]
            for il in range(n_i):
                cb_ref[s * n_i + il, h] = jnp.broadcast_to(cblk[il:il + 1, :], (BF16_ROWS, tm)).astype(BF16)
                ab_ref[s * n_i + il, h] = jnp.broadcast_to(ablk[il:il + 1, :], (BF16_ROWS, tm)).astype(BF16)

    for s in range(n_sub):
        for il in range(n_i):
            for tc in range(tm // BF16_LANES):
                ts = slice(tc * BF16_LANES, (tc + 1) * BF16_LANES)
                gsum = [None] * (PEER_N_KEYS // BF16_ROWS)
                for h in range(PEER_HEADS):
                    cnt_b = cb_ref[s * n_i + il, h, :, ts]
                    ga_b = ab_ref[s * n_i + il, h, :, ts]
                    for jg in range(PEER_N_KEYS // BF16_ROWS):
                        js = slice(jg * BF16_ROWS, (jg + 1) * BF16_ROWS)
                        term = jnp.where(rank_ref[h, js, ts] < cnt_b, gb_ref[h, js, ts] * ga_b, zero_b)
                        gsum[jg] = term if gsum[jg] is None else gsum[jg] + term
                for jg in range(PEER_N_KEYS // BF16_ROWS):
                    rows = slice(il * PEER_N_KEYS + jg * BF16_ROWS, il * PEER_N_KEYS + (jg + 1) * BF16_ROWS)
                    w_ref[s, rows, ts] = gsum[jg] * _gelu(act_ref[s, rows, ts]).astype(BF16)
        acc_ref[...] += _dot(vt_ref[:, s * te:(s + 1) * te], w_ref[s])

    @pl.when(ei == pl.num_programs(1) - 1)
    def _():
        o_ref[...] = x_ref[...] + acc_ref[...].T


def _peer_b(x2, xnt, u_bf, vt_bf, rank, gb, cnt, ga, tm=512, te=1024, n_sub=2):
    t = x2.shape[0]
    sblk = pl.BlockSpec((PEER_HEADS, PEER_N_KEYS, tm), lambda i, e: (0, 0, i))
    return pl.pallas_call(
        functools.partial(_peer_b_kernel, te=te, tm=tm, n_sub=n_sub),
        grid=(t // tm, PEER_N_EXPERTS // (te * n_sub)),
        in_specs=[
            pl.BlockSpec((D_MODEL, tm), lambda i, e: (0, i)),
            pl.BlockSpec((te * n_sub, D_MODEL), lambda i, e: (e, 0)),
            pl.BlockSpec((D_MODEL, te * n_sub), lambda i, e: (0, e)),
            sblk, sblk, sblk, sblk,
            pl.BlockSpec((tm, D_MODEL), lambda i, e: (i, 0)),
        ],
        out_specs=pl.BlockSpec((tm, D_MODEL), lambda i, e: (i, 0)),
        out_shape=jax.ShapeDtypeStruct((t, D_MODEL), F32),
        scratch_shapes=[
            pltpu.VMEM((D_MODEL, tm), F32),
            pltpu.VMEM((n_sub, te, tm), F32),
            pltpu.VMEM((n_sub, te, tm), BF16),
            pltpu.VMEM((n_sub * te // PEER_N_KEYS, PEER_HEADS, BF16_ROWS, tm), BF16),
            pltpu.VMEM((n_sub * te // PEER_N_KEYS, PEER_HEADS, BF16_ROWS, tm), BF16),
        ],
        compiler_params=_cparams(("parallel", "arbitrary")),
        name="peer_b",
    )(xnt, u_bf, vt_bf, rank, gb, cnt, ga, x2)


def _final_norm_kernel(x_ref, g_ref, o_ref):
    o_ref[...] = _rms(x_ref[...], g_ref[...])


def _final_norm(x2, g_row, tm=1024):
    t = x2.shape[0]
    return pl.pallas_call(
        _final_norm_kernel,
        grid=(t // tm,),
        in_specs=[pl.BlockSpec((tm, D_MODEL), lambda i: (i, 0)), pl.BlockSpec((1, D_MODEL), lambda i: (0, 0))],
        out_specs=pl.BlockSpec((tm, D_MODEL), lambda i: (i, 0)),
        out_shape=jax.ShapeDtypeStruct((t, D_MODEL), F32),
        compiler_params=_cparams(("parallel",)),
        name="final_norm",
    )(x2, g_row)


def _peer_layer(x2, norm_g, w_peer_q_l, sub_keys_l, peer_u_l, peer_v_l):
    wq_t = w_peer_q_l.T.astype(BF16)
    keys = sub_keys_l.reshape(2 * PEER_HEADS, PEER_N_KEYS, PEER_HALF).astype(BF16)
    xnt, rank, gb, cnt, ga = _peer_a(x2, norm_g.reshape(1, D_MODEL), wq_t, keys)
    return _peer_b(x2, xnt, peer_u_l.astype(BF16), peer_v_l.T.astype(BF16), rank, gb, cnt, ga)


def kernel(x, norm_mix_g, w_in, w_gla_up, b_gla, b_fox, head_norm_g, w_out, norm_ffn_g,
           w_peer_q, peer_sub_keys, peer_u, peer_v, norm_final_g):
    bsz, seq, _ = x.shape
    depth = w_in.shape[0]
    tq = 256
    dil_tab, dil_cq, dil_ck, dil_back = _dil_consts(seq, tq)
    consts = (_fox_table(tq), dil_tab, dil_cq, dil_ck, dil_back)
    x2 = x.reshape(bsz * seq, D_MODEL)
    for l in range(depth):
        mw = _mixer_weights(w_in[l], w_gla_up[l], b_gla[l], b_fox[l], head_norm_g[l])
        x2, _ = _mixer_layer(x2, bsz, seq, norm_mix_g[l], mw, w_out[l].astype(BF16), consts)
        x2 = _peer_layer(x2, norm_ffn_g[l], w_peer_q[l], peer_sub_keys[l], peer_u[l], peer_v[l])
    return _final_norm(x2, norm_final_g.reshape(1, D_MODEL)).reshape(bsz, seq, D_MODEL)
```

```python
import functools
import math

import numpy as np
import jax
import jax.numpy as jnp
from jax import lax
from jax.experimental import pallas as pl
from jax.experimental.pallas import tpu as pltpu

F32 = jnp.float32
BF16 = jnp.bfloat16

D_MODEL = 1024
HEAD_DIM = 64
N_GROUP_HEADS = 4
GROUP_W = N_GROUP_HEADS * HEAD_DIM
GLA_KEY_DIM = 32
GLA_QK_W = N_GROUP_HEADS * GLA_KEY_DIM
GLA_GATE_RANK = 16
GLA_GATE_TAU = 16.0
GLA_CHUNK = 16
DIL_PAIRS = ((128, 1), (512, 4), (2048, 16))
PEER_HEADS = 8
PEER_N_KEYS = 128
PEER_N_EXPERTS = PEER_N_KEYS * PEER_N_KEYS
PEER_HALF = 128
PEER_TOPK = 16
NORM_EPS = 1e-6

LANES = 128
MXU_DIM = 256
VMEM_LIMIT = 56 * 1024 * 1024

NEG = -1e30

C_SB = 0
C_FOX = 768
C_DIL = 1536
C_GLA_QK = 2304
C_GLA_V = 2560
C_GLA_R = 2816
MAIN_W = 3072
AUX_GD = 0
AUX_FOX = 16


def _cparams(sem, flags=None):
    return pltpu.CompilerParams(dimension_semantics=sem, vmem_limit_bytes=VMEM_LIMIT, flags=flags)


def _split3(x):
    a = x.astype(BF16)
    r = x - a.astype(F32)
    b = r.astype(BF16)
    c = (r - b.astype(F32)).astype(BF16)
    return a, b, c


def _dot(a, b):
    return jnp.dot(a, b, preferred_element_type=F32)


def _dot_nt(a, b):
    return lax.dot_general(a, b, (((1,), (1,)), ((), ())), preferred_element_type=F32)


def _dot_tn(a, b):
    return lax.dot_general(a, b, (((0,), (0,)), ((), ())), preferred_element_type=F32)


def _rms(xf, g):
    ms = jnp.mean(xf * xf, axis=-1, keepdims=True)
    return xf * lax.rsqrt(ms + NORM_EPS) * g


def _mix_in_kernel(x_ref, g_ref, w_ref, wah_ref, wal_ref, main_ref, aux_ref, *, n_chunk):
    y = _rms(x_ref[...], g_ref[...])
    y_hi = y.astype(BF16)
    y_lo = (y - y_hi.astype(F32)).astype(BF16)
    cw = MAIN_W // n_chunk
    for c in range(n_chunk):
        main_ref[:, c * cw:(c + 1) * cw] = _dot(y_hi, w_ref[:, c * cw:(c + 1) * cw]).astype(BF16)
    wah = wah_ref[...]
    aux_ref[...] = _dot(y_hi, wah) + _dot(y_hi, wal_ref[...]) + _dot(y_lo, wah)


def _mix_in(x2, g, w_main, wa_hi, wa_lo, tm=512):
    t = x2.shape[0]
    return pl.pallas_call(
        functools.partial(_mix_in_kernel, n_chunk=6),
        grid=(t // tm,),
        in_specs=[
            pl.BlockSpec((tm, D_MODEL), lambda i: (i, 0)),
            pl.BlockSpec((1, D_MODEL), lambda i: (0, 0)),
            pl.BlockSpec((D_MODEL, MAIN_W), lambda i: (0, 0)),
            pl.BlockSpec((D_MODEL, LANES), lambda i: (0, 0)),
            pl.BlockSpec((D_MODEL, LANES), lambda i: (0, 0)),
        ],
        out_specs=[
            pl.BlockSpec((tm, MAIN_W), lambda i: (i, 0)),
            pl.BlockSpec((tm, LANES), lambda i: (i, 0)),
        ],
        out_shape=[jax.ShapeDtypeStruct((t, MAIN_W), BF16), jax.ShapeDtypeStruct((t, LANES), F32)],
        compiler_params=_cparams(("parallel",)),
        name="mix_in",
    )(x2, g, w_main, wa_hi, wa_lo)


def _fox_prep_kernel(aux_ref, b_ref, ltri_ref, pq_ref, pk_ref, eq_ref, ek_ref, cq_ref, ck_ref, carry_ref):
    @pl.when(pl.program_id(1) == 0)
    def _():
        carry_ref[...] = jnp.zeros_like(carry_ref)

    lf = jax.nn.log_sigmoid(aux_ref[...] + b_ref[...])
    l1, l2, l3 = _split3(lf)
    ltri = ltri_ref[...]
    cf = _dot(ltri, l1) + _dot(ltri, l2) + _dot(ltri, l3) + carry_ref[0:1, :]
    tp = cf.shape[0]
    carry_ref[...] = jnp.broadcast_to(cf[tp - 1:tp, :], carry_ref.shape)
    c1, c2, c3 = _split3(cf)
    cq = _dot(c1, pq_ref[0]) + _dot(c2, pq_ref[1]) + _dot(c3, pq_ref[2]) + eq_ref[...]
    ck = _dot(c1, pk_ref[0]) + _dot(c2, pk_ref[1]) + _dot(c3, pk_ref[2]) + ek_ref[...]
    cq_ref[...] = cq.astype(BF16)
    ck_ref[...] = ck.astype(BF16)


def _aug_col(h, m):
    return (h // 2) * LANES + (h % 2) * 8 + m


def _fox_prep_consts():
    pq = np.zeros((3, LANES, GROUP_W), np.float32)
    pk = np.zeros((3, LANES, GROUP_W), np.float32)
    eq = np.zeros((1, GROUP_W), np.float32)
    ek = np.zeros((1, GROUP_W), np.float32)
    for h in range(N_GROUP_HEADS):
        for m in range(3):
            pq[m, AUX_FOX + h, _aug_col(h, m)] = 1.0
            ek[0, _aug_col(h, m)] = 1.0
            eq[0, _aug_col(h, 3 + m)] = 1.0
            pk[m, AUX_FOX + h, _aug_col(h, 3 + m)] = -1.0
    return jnp.asarray(pq, BF16), jnp.asarray(pk, BF16), jnp.asarray(eq), jnp.asarray(ek)


def _fox_prep(aux, b_row, bsz, seq, tp=256):
    t = aux.shape[0]
    nt = seq // tp
    ltri = jnp.asarray(np.tril(np.ones((tp, tp), np.float32)), BF16)
    pq, pk, eq, ek = _fox_prep_consts()
    const2 = lambda b, i: (0, 0)
    return pl.pallas_call(
        _fox_prep_kernel,
        grid=(bsz, nt),
        in_specs=[
            pl.BlockSpec((tp, LANES), lambda b, i: (b * nt + i, 0)),
            pl.BlockSpec((1, LANES), const2),
            pl.BlockSpec((tp, tp), const2),
            pl.BlockSpec((3, LANES, GROUP_W), lambda b, i: (0, 0, 0)),
            pl.BlockSpec((3, LANES, GROUP_W), lambda b, i: (0, 0, 0)),
            pl.BlockSpec((1, GROUP_W), const2),
            pl.BlockSpec((1, GROUP_W), const2),
        ],
        out_specs=[
            pl.BlockSpec((tp, GROUP_W), lambda b, i: (b * nt + i, 0)),
            pl.BlockSpec((tp, GROUP_W), lambda b, i: (b * nt + i, 0)),
        ],
        out_shape=[jax.ShapeDtypeStruct((t, GROUP_W), BF16), jax.ShapeDtypeStruct((t, GROUP_W), BF16)],
        scratch_shapes=[pltpu.VMEM((8, LANES), F32)],
        compiler_params=_cparams(("arbitrary", "arbitrary")),
        name="fox_prep",
    )(aux, b_row, ltri, pq, pk, eq, ek)


def _attn_kernel(q_ref, cq_ref, k_ref, ck_ref, v_ref, tab_ref, o_ref, acc_ref, *, tq, n_back, n_tab):
    i = pl.program_id(1)
    lane = lax.broadcasted_iota(jnp.int32, (1, LANES), 1)
    n_prev = i if n_back is None else jnp.minimum(i, n_back)
    zero_b = jnp.zeros((), BF16)
    one_b = jnp.ones((), BF16)
    heads = range(N_GROUP_HEADS)

    lhs = []
    for h in heads:
        cs = slice((h // 2) * LANES, (h // 2 + 1) * LANES)
        qm = jnp.where((lane // HEAD_DIM) == h % 2, q_ref[:, cs], zero_b)
        cqm = jnp.where((lane // 8) == h % 2, cq_ref[:, cs], zero_b)
        lhs.append(jnp.concatenate([qm, cqm], axis=-1))

    css = [slice((h // 2) * LANES, (h // 2 + 1) * LANES) for h in heads]

    def logits(j, delta, use_tab):
        r0 = pl.multiple_of(j * tq, tq)
        zs = []
        for h in heads:
            rhs = jnp.concatenate([k_ref[pl.ds(r0, tq), css[h]], ck_ref[pl.ds(r0, tq), css[h]]], axis=-1)
            z = _dot_nt(lhs[h], rhs)
            zs.append(z + tab_ref[delta] if use_tab else z)
        return tuple(zs)

    def consume(j, zs, ms):
        r0 = pl.multiple_of(j * tq, tq)
        new_ms = [jnp.maximum(ms[h], jnp.max(zs[h], axis=-1, keepdims=True)) for h in heads]
        pvs = []
        for h in heads:
            pw = jnp.exp(zs[h] - new_ms[h]).astype(BF16)
            v_aug = jnp.where((lane // HEAD_DIM) == h % 2, v_ref[pl.ds(r0, tq), css[h]], one_b)
            pvs.append(_dot(pw, v_aug))
        for h in heads:
            acc_ref[h] = jnp.exp(ms[h] - new_ms[h]) * acc_ref[h] + pvs[h]
        return tuple(new_ms)

    acc_ref[...] = jnp.zeros_like(acc_ref)
    ms = tuple(jnp.full((tq, 1), NEG, F32) for _ in heads)
    ms = consume(i, logits(i, 0, True), ms)
    lax.fori_loop(1, n_prev + 1, lambda d, c: consume(i - d, logits(i - d, d, n_tab > 1), c), ms)

    for p in range(2):
        a0 = acc_ref[2 * p]
        a1 = acc_ref[2 * p + 1]
        num = jnp.where(lane < HEAD_DIM, a0, a1)
        den = jnp.where(lane < HEAD_DIM, pltpu.roll(a0, HEAD_DIM, 1), pltpu.roll(a1, HEAD_DIM, 1))
        o_ref[:, p * LANES:(p + 1) * LANES] = num / den


def _attn(main, qcol, cq, ck, table, bsz, seq, cq_per_batch, n_back, tq=256):
    t = main.shape[0]
    nq = seq // tq
    n_tab = table.shape[0]
    if cq_per_batch:
        cq_map = lambda b, i: (b * nq + i, 0)
        ck_map = lambda b, i: (b, 0)
    else:
        cq_map = lambda b, i: (i, 0)
        ck_map = lambda b, i: (0, 0)
    qb = qcol // GROUP_W
    return pl.pallas_call(
        functools.partial(_attn_kernel, tq=tq, n_back=n_back, n_tab=n_tab),
        grid=(bsz, nq),
        in_specs=[
            pl.BlockSpec((tq, GROUP_W), lambda b, i: (b * nq + i, qb)),
            pl.BlockSpec((tq, GROUP_W), cq_map),
            pl.BlockSpec((seq, GROUP_W), lambda b, i: (b, qb + 1)),
            pl.BlockSpec((seq, GROUP_W), ck_map),
            pl.BlockSpec((seq, GROUP_W), lambda b, i: (b, qb + 2)),
            pl.BlockSpec((n_tab, tq, tq), lambda b, i: (0, 0, 0)),
        ],
        out_specs=pl.BlockSpec((tq, GROUP_W), lambda b, i: (b * nq + i, 0)),
        out_shape=jax.ShapeDtypeStruct((t, GROUP_W), F32),
        scratch_shapes=[pltpu.VMEM((N_GROUP_HEADS, tq, LANES), F32)],
        compiler_params=_cparams(("parallel", "arbitrary")),
        name="attn",
    )(main, cq, main, ck, main, table)


def _fox_table(tq):
    a = np.arange(tq)
    return jnp.asarray(np.where(a[None, :] <= a[:, None], 0.0, NEG)[None].astype(np.float32))


def _dil_consts(seq, tq):
    max_d = max(w for w, _ in DIL_PAIRS)
    n_back = max_d // tq
    a = np.arange(tq)
    tabs = []
    for delta in range(n_back + 1):
        d = delta * tq + a[:, None] - a[None, :]
        mult = np.zeros_like(d)
        for w, r in DIL_PAIRS:
            mult = mult + ((d >= 0) & (d % r == 0) & (d <= w)).astype(d.dtype)
        tabs.append(np.where(mult > 0, np.log(np.maximum(mult, 1)), NEG))
    table = jnp.asarray(np.stack(tabs).astype(np.float32))
    slopes = 2.0 ** (-8.0 * np.arange(1, N_GROUP_HEADS + 1, dtype=np.float32) / N_GROUP_HEADS)
    pos = np.arange(seq, dtype=np.float32)
    cq = jnp.zeros((seq, GROUP_W), F32)
    ck = jnp.zeros((seq, GROUP_W), F32)
    for h in range(N_GROUP_HEADS):
        bq = jnp.asarray(-slopes[h] * pos)
        pieces_q = _split3(bq)
        pieces_k = _split3(-bq)
        for m in range(3):
            cq = cq.at[:, _aug_col(h, m)].set(pieces_q[m].astype(F32))
            ck = ck.at[:, _aug_col(h, m)].set(1.0)
            cq = cq.at[:, _aug_col(h, 3 + m)].set(1.0)
            ck = ck.at[:, _aug_col(h, 3 + m)].set(pieces_k[m].astype(F32))
    return table, cq.astype(BF16), ck.astype(BF16), n_back


def _sb_kernel(q_ref, k_ref, v_ref, u_ref, o_ref, acc_ref, run_ref, *, tq):
    i = pl.program_id(1)
    lane = lax.broadcasted_iota(jnp.int32, (1, LANES), 1)
    zero_b = jnp.zeros((), BF16)
    row = lax.broadcasted_iota(jnp.int32, (tq, tq), 0)
    col = lax.broadcasted_iota(jnp.int32, (tq, tq), 1)
    strict = col < row
    heads = range(N_GROUP_HEADS)
    qm = [jnp.where((lane // HEAD_DIM) == h % 2, q_ref[:, (h // 2) * LANES:(h // 2 + 1) * LANES], zero_b)
          for h in heads]

    css = [slice((h // 2) * LANES, (h // 2 + 1) * LANES) for h in heads]

    def logits(j, diag):
        r0 = pl.multiple_of(j * tq, tq)
        zs = [_dot_nt(qm[h], k_ref[pl.ds(r0, tq), css[h]]) for h in heads]
        return tuple(jnp.where(strict, z, NEG) for z in zs) if diag else tuple(zs)

    def consume(j, zs):
        r0 = pl.multiple_of(j * tq, tq)
        lbs, rs = [], []
        for h in heads:
            z = zs[h]
            sp = jnp.log(1.0 + jnp.exp(-jnp.abs(z)))
            lb = jnp.minimum(z, 0.0) - sp
            lom = lb - z
            hi = lom.astype(BF16)
            lo = (lom - hi.astype(F32)).astype(BF16)
            lbs.append(lb)
            rs.append(_dot(jnp.concatenate([hi, lo], axis=0), u_ref[...]))
        pvs = []
        for h in heads:
            ts = rs[h][:tq] + rs[h][tq:]
            run = run_ref[h]
            w = jnp.exp(lbs[h] + ts[:, :tq] + jnp.concatenate([run] * (tq // LANES), axis=-1))
            pvs.append(_dot(w.astype(BF16), v_ref[pl.ds(r0, tq), css[h]]))
            run_ref[h] = run + ts[:, tq:]
        for h in heads:
            acc_ref[h] += pvs[h]

    def step(d, zs):
        zs_next = logits(i - d, False)
        consume(i - d + 1, zs)
        return zs_next

    acc_ref[...] = jnp.zeros_like(acc_ref)
    run_ref[...] = jnp.zeros_like(run_ref)
    zs = lax.fori_loop(1, i + 1, step, logits(i, True))
    consume(0, zs)
    for p in range(2):
        o_ref[:, p * LANES:(p + 1) * LANES] = jnp.where(lane < HEAD_DIM, acc_ref[2 * p], acc_ref[2 * p + 1])


def _sb_attn(main, bsz, seq, tq=256):
    t = main.shape[0]
    nq = seq // tq
    a = np.arange(tq)
    u = jnp.asarray(np.concatenate([(a[:, None] > a[None, :]).astype(np.float32),
                                    np.ones((tq, LANES), np.float32)], axis=1), BF16)
    qb = C_SB // GROUP_W
    return pl.pallas_call(
        functools.partial(_sb_kernel, tq=tq),
        grid=(bsz, nq),
        in_specs=[
            pl.BlockSpec((tq, GROUP_W), lambda b, i: (b * nq + i, qb)),
            pl.BlockSpec((seq, GROUP_W), lambda b, i: (b, qb + 1)),
            pl.BlockSpec((seq, GROUP_W), lambda b, i: (b, qb + 2)),
            pl.BlockSpec((tq, tq + LANES), lambda b, i: (0, 0)),
        ],
        out_specs=pl.BlockSpec((tq, GROUP_W), lambda b, i: (b * nq + i, 0)),
        out_shape=jax.ShapeDtypeStruct((t, GROUP_W), F32),
        scratch_shapes=[pltpu.VMEM((N_GROUP_HEADS, tq, LANES), F32),
                        pltpu.VMEM((N_GROUP_HEADS, tq, LANES), F32)],
        compiler_params=_cparams(("parallel", "arbitrary")),
        name="sb_attn",
    )(main, main, main, u)


def _gla_kernel(qk_ref, v_ref, aux_ref, wup_ref, b_ref, lcum_ref, lsum_ref, eh_ref, bd_ref,
                o_ref, state_ref, *, tc):
    @pl.when(pl.program_id(1) == 0)
    def _():
        state_ref[...] = jnp.zeros_like(state_ref)

    c = GLA_CHUNK
    n = tc // c
    q = qk_ref[:, 0:LANES].astype(F32)
    k = qk_ref[:, LANES:2 * LANES].astype(F32)
    v = v_ref[...].astype(F32)
    a1, a2, a3 = _split3(aux_ref[...])
    w1 = wup_ref[0]
    w2 = wup_ref[1]
    logit = (_dot(a1, w1) + _dot(a2, w1) + _dot(a3, w1) + _dot(a1, w2) + _dot(a2, w2)
             + _dot(a1, wup_ref[2]) + b_ref[...])
    log_a = jax.nn.log_sigmoid(logit) * (1.0 / GLA_GATE_TAU)
    s1, s2, s3 = _split3(log_a)
    lcum = lcum_ref[...]
    lsum = lsum_ref[...]
    g = _dot(lcum, s1) + _dot(lcum, s2) + _dot(lcum, s3)
    gl = _dot(lsum, s1) + _dot(lsum, s2) + _dot(lsum, s3)
    q_in = q * jnp.exp(g)
    k_out = k * jnp.exp(gl - g)

    eh = eh_ref[...]
    g3 = g.reshape(n, c, LANES)
    k3 = k.reshape(n, c, LANES)
    q3 = q.reshape(n, c, LANES)
    v3 = v.reshape(n, c, GROUP_W)
    tpos = lax.broadcasted_iota(jnp.int32, (n, c, LANES), 1)
    o3 = jnp.zeros((n, c, GROUP_W), F32)
    for s in range(c):
        dec = jnp.exp(jnp.minimum(g3 - g3[:, s:s + 1, :], 0.0))
        pr = jnp.where(tpos >= s, q3 * k3[:, s:s + 1, :] * dec, 0.0)
        sc = _dot(pr.reshape(tc, LANES).astype(BF16), eh)
        o3 = o3 + sc.reshape(n, c, GROUP_W) * v3[:, s:s + 1, :]
    o_intra = o3.reshape(tc, GROUP_W)

    bd = bd_ref[...]
    st = state_ref[...]
    q_in_b = q_in.astype(BF16)
    k_out_b = k_out.astype(BF16)
    v_b = v_ref[...]
    kvs = [_dot_tn(v_b[ci * c:(ci + 1) * c, :], k_out_b[ci * c:(ci + 1) * c, :]) * bd for ci in range(n)]
    a_all = jnp.exp(gl)
    for ci in range(n):
        rs = slice(ci * c, (ci + 1) * c)
        o_ref[rs, :] = o_intra[rs, :] + _dot_nt(q_in_b[rs, :], st.astype(BF16))
        st = st * a_all[ci * c:ci * c + 1, :] + kvs[ci]
    state_ref[...] = st


def _gla_consts(tc):
    c = GLA_CHUNK
    a = np.arange(tc)
    same = (a[:, None] // c) == (a[None, :] // c)
    lcum = (same & (a[None, :] <= a[:, None])).astype(np.float32)
    lsum = same.astype(np.float32)
    hk = np.arange(LANES) // GLA_KEY_DIM
    hv = np.arange(GROUP_W) // HEAD_DIM
    eh = (hk[:, None] == hv[None, :]).astype(np.float32)
    return (jnp.asarray(lcum, BF16), jnp.asarray(lsum, BF16), jnp.asarray(eh, BF16),
            jnp.asarray(eh.T))


def _gla(main, aux, wup3, b_row, bsz, seq, tc=256):
    t = main.shape[0]
    nt = seq // tc
    lcum, lsum, eh, bd = _gla_consts(tc)
    const2 = lambda b, i: (0, 0)
    return pl.pallas_call(
        functools.partial(_gla_kernel, tc=tc),
        grid=(bsz, nt),
        in_specs=[
            pl.BlockSpec((tc, 2 * LANES), lambda b, i: (b * nt + i, C_GLA_QK // (2 * LANES))),
            pl.BlockSpec((tc, GROUP_W), lambda b, i: (b * nt + i, C_GLA_V // GROUP_W)),
            pl.BlockSpec((tc, LANES), lambda b, i: (b * nt + i, 0)),
            pl.BlockSpec((3, LANES, LANES), lambda b, i: (0, 0, 0)),
            pl.BlockSpec((1, LANES), const2),
            pl.BlockSpec((tc, tc), const2),
            pl.BlockSpec((tc, tc), const2),
            pl.BlockSpec((LANES, GROUP_W), const2),
            pl.BlockSpec((GROUP_W, LANES), const2),
        ],
        out_specs=pl.BlockSpec((tc, GROUP_W), lambda b, i: (b * nt + i, 0)),
        out_shape=jax.ShapeDtypeStruct((t, GROUP_W), F32),
        scratch_shapes=[pltpu.VMEM((GROUP_W, LANES), F32)],
        compiler_params=_cparams(("parallel", "arbitrary")),
        name="gla",
    )(main, main, aux, wup3, b_row, lcum, lsum, eh, bd)


def _mix_out_kernel(x_ref, osb_ref, ogla_ref, odil_ref, ofox_ref, r_ref, g_ref, eavg_ref, w_ref, xo_ref):
    eavg = eavg_ref[...]
    acc = x_ref[...]
    for gi, o_ref in enumerate((osb_ref, ogla_ref, odil_ref, ofox_ref)):
        o = o_ref[...]
        sq = o * o
        hi = sq.astype(BF16)
        lo = (sq - hi.astype(F32)).astype(BF16)
        ms = _dot(hi, eavg) + _dot(lo, eavg)
        y = o * lax.rsqrt(ms + NORM_EPS) * g_ref[:, gi * GROUP_W:(gi + 1) * GROUP_W]
        if gi == 1:
            r = r_ref[...].astype(F32)
            y = y * (r * jax.nn.sigmoid(r))
        acc = acc + _dot(y.astype(BF16), w_ref[gi * GROUP_W:(gi + 1) * GROUP_W, :])
    xo_ref[...] = acc


def _mix_out(x2, o_sb, o_gla, o_dil, o_fox, main, g_row, w_out, tm=512):
    t = x2.shape[0]
    hv = np.arange(GROUP_W) // HEAD_DIM
    eavg = jnp.asarray((hv[:, None] == hv[None, :]).astype(np.float32) / HEAD_DIM, BF16)
    ob = pl.BlockSpec((tm, GROUP_W), lambda i: (i, 0))
    return pl.pallas_call(
        _mix_out_kernel,
        grid=(t // tm,),
        in_specs=[
            pl.BlockSpec((tm, D_MODEL), lambda i: (i, 0)),
            ob, ob, ob, ob,
            pl.BlockSpec((tm, GROUP_W), lambda i: (i, C_GLA_R // GROUP_W)),
            pl.BlockSpec((1, D_MODEL), lambda i: (0, 0)),
            pl.BlockSpec((GROUP_W, GROUP_W), lambda i: (0, 0)),
            pl.BlockSpec((D_MODEL, D_MODEL), lambda i: (0, 0)),
        ],
        out_specs=pl.BlockSpec((tm, D_MODEL), lambda i: (i, 0)),
        out_shape=jax.ShapeDtypeStruct((t, D_MODEL), F32),
        compiler_params=_cparams(("parallel",)),
        name="mix_out",
    )(x2, o_sb, o_gla, o_dil, o_fox, main, g_row, eavg, w_out)


def _mixer_weights(w_in_l, w_gla_up_l, b_gla_l, b_fox_l, head_g_l):
    splits = (GROUP_W,) * 3 + (GLA_QK_W, GLA_QK_W, GROUP_W, GROUP_W, GLA_GATE_RANK) + (GROUP_W,) * 6 + (N_GROUP_HEADS,)
    cols = []
    start = 0
    for w in splits:
        cols.append(w_in_l[:, start:start + w])
        start += w
    (sb_q, sb_k, sb_v, gla_q, gla_k, gla_v, gla_r, gla_gd,
     dil_q, dil_k, dil_v, fox_q, fox_k, fox_v, fox_f) = cols
    hs = HEAD_DIM ** -0.5
    w_main = jnp.concatenate(
        [sb_q * hs, sb_k, sb_v, fox_q * hs, fox_k, fox_v, dil_q * hs, dil_k, dil_v,
         gla_q * (GLA_KEY_DIM ** -0.5), gla_k, gla_v, gla_r], axis=1).astype(BF16)
    w_aux = jnp.concatenate(
        [gla_gd, fox_f, jnp.zeros((D_MODEL, LANES - GLA_GATE_RANK - N_GROUP_HEADS), F32)], axis=1)
    wa_hi = w_aux.astype(BF16)
    wa_lo = (w_aux - wa_hi.astype(F32)).astype(BF16)
    wup = jnp.zeros((LANES, LANES), F32).at[:GLA_GATE_RANK, :].set(w_gla_up_l)
    wup3 = jnp.stack([p for p in _split3(wup)])
    b_gla_row = b_gla_l.reshape(1, LANES)
    b_fox_row = jnp.zeros((1, LANES), F32).at[0, AUX_FOX:AUX_FOX + N_GROUP_HEADS].set(b_fox_l)
    g_row = head_g_l.reshape(1, D_MODEL)
    return w_main, wa_hi, wa_lo, wup3, b_gla_row, b_fox_row, g_row


def _mixer_layer(x2, bsz, seq, norm_g, mw, w_out_bf, consts):
    w_main, wa_hi, wa_lo, wup3, b_gla_row, b_fox_row, g_row = mw
    fox_tab, dil_tab, dil_cq, dil_ck, dil_back = consts
    main, aux = _mix_in(x2, norm_g.reshape(1, D_MODEL), w_main, wa_hi, wa_lo)
    o_sb = _sb_attn(main, bsz, seq)
    o_gla = _gla(main, aux, wup3, b_gla_row, bsz, seq)
    o_dil = _attn(main, C_DIL, dil_cq, dil_ck, dil_tab, bsz, seq, False, dil_back)
    fcq, fck = _fox_prep(aux, b_fox_row, bsz, seq)
    o_fox = _attn(main, C_FOX, fcq, fck, fox_tab, bsz, seq, True, None)
    x2n = _mix_out(x2, o_sb, o_gla, o_dil, o_fox, main, g_row, w_out_bf)
    return x2n, (o_sb, o_gla, o_dil, o_fox)


def _top_rows(x, k):
    rows = []
    for _ in range(k):
        m = jnp.max(x, axis=0, keepdims=True)
        rows.append(m)
        x = jnp.where(x >= m, NEG, x)
    return rows


def _top_rows_rank(x, k):
    rows = []
    rank = jnp.full(x.shape, float(k), F32)
    for r in range(k):
        m = jnp.max(x, axis=0, keepdims=True)
        rows.append(m)
        hit = x >= m
        rank = jnp.where(hit, float(r), rank)
        x = jnp.where(hit, NEG, x)
    return rows, rank


_PEER_PAIRS = [(p, q) for p in range(PEER_TOPK) for q in range(PEER_TOPK) if (p + 1) * (q + 1) <= PEER_TOPK]
_PEER_CAND_ROWS = -(-len(_PEER_PAIRS) // 16) * 16


def _peer_a_kernel(x_ref, g_ref, wq_ref, keys_ref, seg_ref, xnt_ref, rank_ref, gb_ref, cnt_ref, ga_ref):
    xnt = _rms(x_ref[...], g_ref[...]).T.astype(BF16)
    xnt_ref[...] = xnt
    qt = _dot(wq_ref[...], xnt).astype(BF16)
    tm = xnt.shape[1]
    pad = _PEER_CAND_ROWS - len(_PEER_PAIRS)
    for h in range(PEER_HEADS):
        st1 = _dot(keys_ref[2 * h], qt[(2 * h) * PEER_HALF:(2 * h + 1) * PEER_HALF, :])
        st2 = _dot(keys_ref[2 * h + 1], qt[(2 * h + 1) * PEER_HALF:(2 * h + 2) * PEER_HALF, :])
        a_rows = _top_rows(st1, PEER_TOPK)
        b_rows, rank2 = _top_rows_rank(st2, PEER_TOPK)
        cand = jnp.concatenate([a_rows[p] + b_rows[q] for p, q in _PEER_PAIRS]
                               + [jnp.full((pad, tm), NEG, F32)], axis=0)
        best = _top_rows(cand, PEER_TOPK)
        zsum = jnp.zeros((1, tm), F32)
        for r in range(PEER_TOPK):
            zsum = zsum + jnp.exp(best[r] - best[0])
        sel = jnp.where(cand >= best[PEER_TOPK - 1], 1.0, 0.0).astype(BF16)
        cnt_rows = _dot(seg_ref[...], sel)
        cnt = jnp.zeros_like(st1)
        for p in range(PEER_TOPK):
            cnt = jnp.where(st1 == a_rows[p], cnt_rows[p:p + 1, :], cnt)
        rank_ref[h] = rank2.astype(BF16)
        gb_ref[h] = jnp.exp(st2 - b_rows[0]).astype(BF16)
        cnt_ref[h] = cnt
        ga_ref[h] = jnp.exp(st1 - (a_rows[0] + jnp.log(zsum)))


def _peer_a(x2, g_row, wq_t, keys, tm=512):
    t = x2.shape[0]
    seg = np.zeros((PEER_TOPK, _PEER_CAND_ROWS), np.float32)
    for r, (p, _) in enumerate(_PEER_PAIRS):
        seg[p, r] = 1.0
    sblk = pl.BlockSpec((PEER_HEADS, PEER_N_KEYS, tm), lambda i: (0, 0, i))
    return pl.pallas_call(
        _peer_a_kernel,
        grid=(t // tm,),
        in_specs=[
            pl.BlockSpec((tm, D_MODEL), lambda i: (i, 0)),
            pl.BlockSpec((1, D_MODEL), lambda i: (0, 0)),
            pl.BlockSpec((2 * PEER_HEADS * PEER_HALF, D_MODEL), lambda i: (0, 0)),
            pl.BlockSpec((2 * PEER_HEADS, PEER_N_KEYS, PEER_HALF), lambda i: (0, 0, 0)),
            pl.BlockSpec((PEER_TOPK, _PEER_CAND_ROWS), lambda i: (0, 0)),
        ],
        out_specs=[pl.BlockSpec((D_MODEL, tm), lambda i: (0, i)), sblk, sblk, sblk, sblk],
        out_shape=[
            jax.ShapeDtypeStruct((D_MODEL, t), BF16),
            jax.ShapeDtypeStruct((PEER_HEADS, PEER_N_KEYS, t), BF16),
            jax.ShapeDtypeStruct((PEER_HEADS, PEER_N_KEYS, t), BF16),
            jax.ShapeDtypeStruct((PEER_HEADS, PEER_N_KEYS, t), F32),
            jax.ShapeDtypeStruct((PEER_HEADS, PEER_N_KEYS, t), F32),
        ],
        compiler_params=_cparams(("parallel",)),
        name="peer_a",
    )(x2, g_row, wq_t, keys, jnp.asarray(seg, BF16))


def _gelu(a):
    return 0.5 * a * (1.0 + lax.erf(a * (2.0 ** -0.5)))


BF16_ROWS = 8
BF16_LANES = 2 * LANES


def _peer_b_kernel(xnt_ref, u_ref, vt_ref, rank_ref, gb_ref, cnt_ref, ga_ref, x_ref, o_ref,
                   acc_ref, act_ref, w_ref, cb_ref, ab_ref, *, te, tm, n_sub):
    ei = pl.program_id(1)
    n_i = te // PEER_N_KEYS
    zero_b = jnp.zeros((), BF16)

    @pl.when(ei == 0)
    def _():
        acc_ref[...] = jnp.zeros_like(acc_ref)

    n_rows = n_sub * n_i
    i0 = pl.multiple_of(ei * n_rows, n_rows)
    for h in range(PEER_HEADS):
        cblk = cnt_ref[h, pl.ds(i0, n_rows), :]
        ablk = ga_ref[h, pl.ds(i0, n_rows), :]
        for r in range(n_rows):
            cb_ref[r, h] = jnp.broadcast_to(cblk[r:r + 1, :], (BF16_ROWS, tm)).astype(BF16)
            ab_ref[r, h] = jnp.broadcast_to(ablk[r:r + 1, :], (BF16_ROWS, tm)).astype(BF16)

    def act_dot(s):
        act_ref[s] = _dot(u_ref[s * te:(s + 1) * te, :], xnt_ref[...])

    def weights(s):
        for il in range(n_i):
            for tc in range(tm // BF16_LANES):
                ts = slice(tc * BF16_LANES, (tc + 1) * BF16_LANES)
                gsum = [None] * (PEER_N_KEYS // BF16_ROWS)
                for h in range(PEER_HEADS):
                    cnt_b = cb_ref[s * n_i + il, h, :, ts]
                    ga_b = ab_ref[s * n_i + il, h, :, ts]
                    for jg in range(PEER_N_KEYS // BF16_ROWS):
                        js = slice(jg * BF16_ROWS, (jg + 1) * BF16_ROWS)
                        term = jnp.where(rank_ref[h, js, ts] < cnt_b, gb_ref[h, js, ts] * ga_b, zero_b)
                        gsum[jg] = term if gsum[jg] is None else gsum[jg] + term
                for jg in range(PEER_N_KEYS // BF16_ROWS):
                    rows = slice(il * PEER_N_KEYS + jg * BF16_ROWS, il * PEER_N_KEYS + (jg + 1) * BF16_ROWS)
                    w_ref[s, rows, ts] = gsum[jg] * _gelu(act_ref[s, rows, ts]).astype(BF16)

    def out_dot(s):
        acc_ref[...] += _dot(vt_ref[:, s * te:(s + 1) * te], w_ref[s])

    for k in range(n_sub):
        act_dot(k)
    for k in range(n_sub):
        weights(k)
        out_dot(k)

    @pl.when(ei == pl.num_programs(1) - 1)
    def _():
        o_ref[...] = x_ref[...] + acc_ref[...].T


def _peer_b(x2, xnt, u_bf, vt_bf, rank, gb, cnt, ga, tm=512, te=1024, n_sub=2):
    t = x2.shape[0]
    sblk = pl.BlockSpec((PEER_HEADS, PEER_N_KEYS, tm), lambda i, e: (0, 0, i))
    return pl.pallas_call(
        functools.partial(_peer_b_kernel, te=te, tm=tm, n_sub=n_sub),
        grid=(t // tm, PEER_N_EXPERTS // (te * n_sub)),
        in_specs=[
            pl.BlockSpec((D_MODEL, tm), lambda i, e: (0, i)),
            pl.BlockSpec((te * n_sub, D_MODEL), lambda i, e: (e, 0)),
            pl.BlockSpec((D_MODEL, te * n_sub), lambda i, e: (0, e)),
            sblk, sblk, sblk, sblk,
            pl.BlockSpec((tm, D_MODEL), lambda i, e: (i, 0)),
        ],
        out_specs=pl.BlockSpec((tm, D_MODEL), lambda i, e: (i, 0)),
        out_shape=jax.ShapeDtypeStruct((t, D_MODEL), F32),
        scratch_shapes=[
            pltpu.VMEM((D_MODEL, tm), F32),
            pltpu.VMEM((n_sub, te, tm), F32),
            pltpu.VMEM((n_sub, te, tm), BF16),
            pltpu.VMEM((n_sub * te // PEER_N_KEYS, PEER_HEADS, BF16_ROWS, tm), BF16),
            pltpu.VMEM((n_sub * te // PEER_N_KEYS, PEER_HEADS, BF16_ROWS, tm), BF16),
        ],
        compiler_params=_cparams(("parallel", "arbitrary")),
        name="peer_b",
    )(xnt, u_bf, vt_bf, rank, gb, cnt, ga, x2)


def _final_norm_kernel(x_ref, g_ref, o_ref):
    o_ref[...] = _rms(x_ref[...], g_ref[...])


def _final_norm(x2, g_row, tm=1024):
    t = x2.shape[0]
    return pl.pallas_call(
        _final_norm_kernel,
        grid=(t // tm,),
        in_specs=[pl.BlockSpec((tm, D_MODEL), lambda i: (i, 0)), pl.BlockSpec((1, D_MODEL), lambda i: (0, 0))],
        out_specs=pl.BlockSpec((tm, D_MODEL), lambda i: (i, 0)),
        out_shape=jax.ShapeDtypeStruct((t, D_MODEL), F32),
        compiler_params=_cparams(("parallel",)),
        name="final_norm",
    )(x2, g_row)


def _peer_layer(x2, norm_g, w_peer_q_l, sub_keys_l, peer_u_l, peer_v_l):
    wq_t = w_peer_q_l.T.astype(BF16)
    keys = sub_keys_l.reshape(2 * PEER_HEADS, PEER_N_KEYS, PEER_HALF).astype(BF16)
    xnt, rank, gb, cnt, ga = _peer_a(x2, norm_g.reshape(1, D_MODEL), wq_t, keys)
    return _peer_b(x2, xnt, peer_u_l.astype(BF16), peer_v_l.T.astype(BF16), rank, gb, cnt, ga)


def kernel(x, norm_mix_g, w_in, w_gla_up, b_gla, b_fox, head_norm_g, w_out, norm_ffn_g,
           w_peer_q, peer_sub_keys, peer_u, peer_v, norm_final_g):
    bsz, seq, _ = x.shape
    depth = w_in.shape[0]
    tq = 256
    dil_tab, dil_cq, dil_ck, dil_back = _dil_consts(seq, tq)
    consts = (_fox_table(tq), dil_tab, dil_cq, dil_ck, dil_back)
    x2 = x.reshape(bsz * seq, D_MODEL)
    for l in range(depth):
        mw = _mixer_weights(w_in[l], w_gla_up[l], b_gla[l], b_fox[l], head_norm_g[l])
        x2, _ = _mixer_layer(x2, bsz, seq, norm_mix_g[l], mw, w_out[l].astype(BF16), consts)
        x2 = _peer_layer(x2, norm_ffn_g[l], w_peer_q[l], peer_sub_keys[l], peer_u[l], peer_v[l])
    return _final_norm(x2, norm_final_g.reshape(1, D_MODEL)).reshape(bsz, seq, D_MODEL)
```

```python
import functools
import math

import numpy as np
import jax
import jax.numpy as jnp
from jax import lax
from jax.experimental import pallas as pl
from jax.experimental.pallas import tpu as pltpu

F32 = jnp.float32
BF16 = jnp.bfloat16

D_MODEL = 1024
HEAD_DIM = 64
N_GROUP_HEADS = 4
GROUP_W = N_GROUP_HEADS * HEAD_DIM
GLA_KEY_DIM = 32
GLA_QK_W = N_GROUP_HEADS * GLA_KEY_DIM
GLA_GATE_RANK = 16
GLA_GATE_TAU = 16.0
GLA_CHUNK = 16
DIL_PAIRS = ((128, 1), (512, 4), (2048, 16))
PEER_HEADS = 8
PEER_N_KEYS = 128
PEER_N_EXPERTS = PEER_N_KEYS * PEER_N_KEYS
PEER_HALF = 128
PEER_TOPK = 16
NORM_EPS = 1e-6

LANES = 128
MXU_DIM = 256
VMEM_LIMIT = 56 * 1024 * 1024

NEG = -1e30

C_SB = 0
C_FOX = 768
C_DIL = 1536
C_GLA_QK = 2304
C_GLA_V = 2560
C_GLA_R = 2816
MAIN_W = 3072
AUX_GD = 0
AUX_FOX = 16


def _cparams(sem, flags=None):
    return pltpu.CompilerParams(dimension_semantics=sem, vmem_limit_bytes=VMEM_LIMIT, flags=flags)


def _split3(x):
    a = x.astype(BF16)
    r = x - a.astype(F32)
    b = r.astype(BF16)
    c = (r - b.astype(F32)).astype(BF16)
    return a, b, c


def _dot(a, b):
    return jnp.dot(a, b, preferred_element_type=F32)


def _dot_nt(a, b):
    return lax.dot_general(a, b, (((1,), (1,)), ((), ())), preferred_element_type=F32)


def _dot_tn(a, b):
    return lax.dot_general(a, b, (((0,), (0,)), ((), ())), preferred_element_type=F32)


def _rms(xf, g):
    ms = jnp.mean(xf * xf, axis=-1, keepdims=True)
    return xf * lax.rsqrt(ms + NORM_EPS) * g


def _mix_in_kernel(x_ref, g_ref, w_ref, wah_ref, wal_ref, main_ref, aux_ref, *, n_chunk):
    y = _rms(x_ref[...], g_ref[...])
    y_hi = y.astype(BF16)
    y_lo = (y - y_hi.astype(F32)).astype(BF16)
    cw = MAIN_W // n_chunk
    for c in range(n_chunk):
        main_ref[:, c * cw:(c + 1) * cw] = _dot(y_hi, w_ref[:, c * cw:(c + 1) * cw]).astype(BF16)
    wah = wah_ref[...]
    aux_ref[...] = _dot(y_hi, wah) + _dot(y_hi, wal_ref[...]) + _dot(y_lo, wah)


def _mix_in(x2, g, w_main, wa_hi, wa_lo, tm=512):
    t = x2.shape[0]
    return pl.pallas_call(
        functools.partial(_mix_in_kernel, n_chunk=6),
        grid=(t // tm,),
        in_specs=[
            pl.BlockSpec((tm, D_MODEL), lambda i: (i, 0)),
            pl.BlockSpec((1, D_MODEL), lambda i: (0, 0)),
            pl.BlockSpec((D_MODEL, MAIN_W), lambda i: (0, 0)),
            pl.BlockSpec((D_MODEL, LANES), lambda i: (0, 0)),
            pl.BlockSpec((D_MODEL, LANES), lambda i: (0, 0)),
        ],
        out_specs=[
            pl.BlockSpec((tm, MAIN_W), lambda i: (i, 0)),
            pl.BlockSpec((tm, LANES), lambda i: (i, 0)),
        ],
        out_shape=[jax.ShapeDtypeStruct((t, MAIN_W), BF16), jax.ShapeDtypeStruct((t, LANES), F32)],
        compiler_params=_cparams(("parallel",)),
        name="mix_in",
    )(x2, g, w_main, wa_hi, wa_lo)


def _fox_prep_kernel(aux_ref, b_ref, ltri_ref, pq_ref, pk_ref, eq_ref, ek_ref, cq_ref, ck_ref, carry_ref):
    @pl.when(pl.program_id(1) == 0)
    def _():
        carry_ref[...] = jnp.zeros_like(carry_ref)

    lf = jax.nn.log_sigmoid(aux_ref[...] + b_ref[...])
    l1, l2, l3 = _split3(lf)
    ltri = ltri_ref[...]
    cf = _dot(ltri, l1) + _dot(ltri, l2) + _dot(ltri, l3) + carry_ref[0:1, :]
    tp = cf.shape[0]
    carry_ref[...] = jnp.broadcast_to(cf[tp - 1:tp, :], carry_ref.shape)
    c1, c2, c3 = _split3(cf)
    cq = _dot(c1, pq_ref[0]) + _dot(c2, pq_ref[1]) + _dot(c3, pq_ref[2]) + eq_ref[...]
    ck = _dot(c1, pk_ref[0]) + _dot(c2, pk_ref[1]) + _dot(c3, pk_ref[2]) + ek_ref[...]
    cq_ref[...] = cq.astype(BF16)
    ck_ref[...] = ck.astype(BF16)


def _aug_col(h, m):
    return (h // 2) * LANES + (h % 2) * 8 + m


def _fox_prep_consts():
    pq = np.zeros((3, LANES, GROUP_W), np.float32)
    pk = np.zeros((3, LANES, GROUP_W), np.float32)
    eq = np.zeros((1, GROUP_W), np.float32)
    ek = np.zeros((1, GROUP_W), np.float32)
    for h in range(N_GROUP_HEADS):
        for m in range(3):
            pq[m, AUX_FOX + h, _aug_col(h, m)] = 1.0
            ek[0, _aug_col(h, m)] = 1.0
            eq[0, _aug_col(h, 3 + m)] = 1.0
            pk[m, AUX_FOX + h, _aug_col(h, 3 + m)] = -1.0
    return jnp.asarray(pq, BF16), jnp.asarray(pk, BF16), jnp.asarray(eq), jnp.asarray(ek)


def _fox_prep(aux, b_row, bsz, seq, tp=256):
    t = aux.shape[0]
    nt = seq // tp
    ltri = jnp.asarray(np.tril(np.ones((tp, tp), np.float32)), BF16)
    pq, pk, eq, ek = _fox_prep_consts()
    const2 = lambda b, i: (0, 0)
    return pl.pallas_call(
        _fox_prep_kernel,
        grid=(bsz, nt),
        in_specs=[
            pl.BlockSpec((tp, LANES), lambda b, i: (b * nt + i, 0)),
            pl.BlockSpec((1, LANES), const2),
            pl.BlockSpec((tp, tp), const2),
            pl.BlockSpec((3, LANES, GROUP_W), lambda b, i: (0, 0, 0)),
            pl.BlockSpec((3, LANES, GROUP_W), lambda b, i: (0, 0, 0)),
            pl.BlockSpec((1, GROUP_W), const2),
            pl.BlockSpec((1, GROUP_W), const2),
        ],
        out_specs=[
            pl.BlockSpec((tp, GROUP_W), lambda b, i: (b * nt + i, 0)),
            pl.BlockSpec((tp, GROUP_W), lambda b, i: (b * nt + i, 0)),
        ],
        out_shape=[jax.ShapeDtypeStruct((t, GROUP_W), BF16), jax.ShapeDtypeStruct((t, GROUP_W), BF16)],
        scratch_shapes=[pltpu.VMEM((8, LANES), F32)],
        compiler_params=_cparams(("arbitrary", "arbitrary")),
        name="fox_prep",
    )(aux, b_row, ltri, pq, pk, eq, ek)


def _attn_kernel(q_ref, cq_ref, k_ref, ck_ref, v_ref, tab_ref, o_ref, acc_ref, *, tq, n_back, n_tab):
    i = pl.program_id(1)
    lane = lax.broadcasted_iota(jnp.int32, (1, LANES), 1)
    n_prev = i if n_back is None else jnp.minimum(i, n_back)
    zero_b = jnp.zeros((), BF16)
    one_b = jnp.ones((), BF16)
    heads = range(N_GROUP_HEADS)

    lhs = []
    for h in heads:
        cs = slice((h // 2) * LANES, (h // 2 + 1) * LANES)
        qm = jnp.where((lane // HEAD_DIM) == h % 2, q_ref[:, cs], zero_b)
        cqm = jnp.where((lane // 8) == h % 2, cq_ref[:, cs], zero_b)
        lhs.append(jnp.concatenate([qm, cqm], axis=-1))

    css = [slice((h // 2) * LANES, (h // 2 + 1) * LANES) for h in heads]

    def logits(j, delta, use_tab):
        r0 = pl.multiple_of(j * tq, tq)
        zs = []
        for h in heads:
            rhs = jnp.concatenate([k_ref[pl.ds(r0, tq), css[h]], ck_ref[pl.ds(r0, tq), css[h]]], axis=-1)
            z = _dot_nt(lhs[h], rhs)
            zs.append(z + tab_ref[delta] if use_tab else z)
        return tuple(zs)

    def consume(j, zs, ms):
        r0 = pl.multiple_of(j * tq, tq)
        new_ms = [jnp.maximum(ms[h], jnp.max(zs[h], axis=-1, keepdims=True)) for h in heads]
        pvs = []
        for h in heads:
            pw = jnp.exp(zs[h] - new_ms[h]).astype(BF16)
            v_aug = jnp.where((lane // HEAD_DIM) == h % 2, v_ref[pl.ds(r0, tq), css[h]], one_b)
            pvs.append(_dot(pw, v_aug))
        for h in heads:
            acc_ref[h] = jnp.exp(ms[h] - new_ms[h]) * acc_ref[h] + pvs[h]
        return tuple(new_ms)

    acc_ref[...] = jnp.zeros_like(acc_ref)
    ms = tuple(jnp.full((tq, 1), NEG, F32) for _ in heads)
    ms = consume(i, logits(i, 0, True), ms)
    lax.fori_loop(1, n_prev + 1, lambda d, c: consume(i - d, logits(i - d, d, n_tab > 1), c), ms)

    for p in range(2):
        a0 = acc_ref[2 * p]
        a1 = acc_ref[2 * p + 1]
        num = jnp.where(lane < HEAD_DIM, a0, a1)
        den = jnp.where(lane < HEAD_DIM, pltpu.roll(a0, HEAD_DIM, 1), pltpu.roll(a1, HEAD_DIM, 1))
        o_ref[:, p * LANES:(p + 1) * LANES] = num / den


def _attn(main, qcol, cq, ck, table, bsz, seq, cq_per_batch, n_back, tq=256):
    t = main.shape[0]
    nq = seq // tq
    n_tab = table.shape[0]
    if cq_per_batch:
        cq_map = lambda b, i: (b * nq + i, 0)
        ck_map = lambda b, i: (b, 0)
    else:
        cq_map = lambda b, i: (i, 0)
        ck_map = lambda b, i: (0, 0)
    qb = qcol // GROUP_W
    return pl.pallas_call(
        functools.partial(_attn_kernel, tq=tq, n_back=n_back, n_tab=n_tab),
        grid=(bsz, nq),
        in_specs=[
            pl.BlockSpec((tq, GROUP_W), lambda b, i: (b * nq + i, qb)),
            pl.BlockSpec((tq, GROUP_W), cq_map),
            pl.BlockSpec((seq, GROUP_W), lambda b, i: (b, qb + 1)),
            pl.BlockSpec((seq, GROUP_W), ck_map),
            pl.BlockSpec((seq, GROUP_W), lambda b, i: (b, qb + 2)),
            pl.BlockSpec((n_tab, tq, tq), lambda b, i: (0, 0, 0)),
        ],
        out_specs=pl.BlockSpec((tq, GROUP_W), lambda b, i: (b * nq + i, 0)),
        out_shape=jax.ShapeDtypeStruct((t, GROUP_W), F32),
        scratch_shapes=[pltpu.VMEM((N_GROUP_HEADS, tq, LANES), F32)],
        compiler_params=_cparams(("parallel", "arbitrary")),
        name="attn",
    )(main, cq, main, ck, main, table)


def _fox_table(tq):
    a = np.arange(tq)
    return jnp.asarray(np.where(a[None, :] <= a[:, None], 0.0, NEG)[None].astype(np.float32))


def _dil_consts(seq, tq):
    max_d = max(w for w, _ in DIL_PAIRS)
    n_back = max_d // tq
    a = np.arange(tq)
    tabs = []
    for delta in range(n_back + 1):
        d = delta * tq + a[:, None] - a[None, :]
        mult = np.zeros_like(d)
        for w, r in DIL_PAIRS:
            mult = mult + ((d >= 0) & (d % r == 0) & (d <= w)).astype(d.dtype)
        tabs.append(np.where(mult > 0, np.log(np.maximum(mult, 1)), NEG))
    table = jnp.asarray(np.stack(tabs).astype(np.float32))
    slopes = 2.0 ** (-8.0 * np.arange(1, N_GROUP_HEADS + 1, dtype=np.float32) / N_GROUP_HEADS)
    pos = np.arange(seq, dtype=np.float32)
    cq = jnp.zeros((seq, GROUP_W), F32)
    ck = jnp.zeros((seq, GROUP_W), F32)
    for h in range(N_GROUP_HEADS):
        bq = jnp.asarray(-slopes[h] * pos)
        pieces_q = _split3(bq)
        pieces_k = _split3(-bq)
        for m in range(3):
            cq = cq.at[:, _aug_col(h, m)].set(pieces_q[m].astype(F32))
            ck = ck.at[:, _aug_col(h, m)].set(1.0)
            cq = cq.at[:, _aug_col(h, 3 + m)].set(1.0)
            ck = ck.at[:, _aug_col(h, 3 + m)].set(pieces_k[m].astype(F32))
    return table, cq.astype(BF16), ck.astype(BF16), n_back


def _sb_kernel(q_ref, k_ref, v_ref, u_ref, o_ref, acc_ref, run_ref, *, tq):
    i = pl.program_id(1)
    lane = lax.broadcasted_iota(jnp.int32, (1, LANES), 1)
    zero_b = jnp.zeros((), BF16)
    row = lax.broadcasted_iota(jnp.int32, (tq, tq), 0)
    col = lax.broadcasted_iota(jnp.int32, (tq, tq), 1)
    strict = col < row
    heads = range(N_GROUP_HEADS)
    qm = [jnp.where((lane // HEAD_DIM) == h % 2, q_ref[:, (h // 2) * LANES:(h // 2 + 1) * LANES], zero_b)
          for h in heads]

    css = [slice((h // 2) * LANES, (h // 2 + 1) * LANES) for h in heads]

    def logits(j, diag):
        r0 = pl.multiple_of(j * tq, tq)
        zs = [_dot_nt(qm[h], k_ref[pl.ds(r0, tq), css[h]]) for h in heads]
        return tuple(jnp.where(strict, z, NEG) for z in zs) if diag else tuple(zs)

    def consume(j, zs):
        r0 = pl.multiple_of(j * tq, tq)
        lbs, rs = [], []
        for h in heads:
            z = zs[h]
            sp = jnp.log(1.0 + jnp.exp(-jnp.abs(z)))
            lb = jnp.minimum(z, 0.0) - sp
            lom = lb - z
            hi = lom.astype(BF16)
            lo = (lom - hi.astype(F32)).astype(BF16)
            lbs.append(lb)
            rs.append(_dot(jnp.concatenate([hi, lo], axis=0), u_ref[...]))
        pvs = []
        for h in heads:
            ts = rs[h][:tq] + rs[h][tq:]
            run = run_ref[h]
            w = jnp.exp(lbs[h] + ts[:, :tq] + jnp.concatenate([run] * (tq // LANES), axis=-1))
            pvs.append(_dot(w.astype(BF16), v_ref[pl.ds(r0, tq), css[h]]))
            run_ref[h] = run + ts[:, tq:]
        for h in heads:
            acc_ref[h] += pvs[h]

    def step(d, zs):
        zs_next = logits(i - d, False)
        consume(i - d + 1, zs)
        return zs_next

    acc_ref[...] = jnp.zeros_like(acc_ref)
    run_ref[...] = jnp.zeros_like(run_ref)
    zs = lax.fori_loop(1, i + 1, step, logits(i, True))
    consume(0, zs)
    for p in range(2):
        o_ref[:, p * LANES:(p + 1) * LANES] = jnp.where(lane < HEAD_DIM, acc_ref[2 * p], acc_ref[2 * p + 1])


def _sb_attn(main, bsz, seq, tq=256):
    t = main.shape[0]
    nq = seq // tq
    a = np.arange(tq)
    u = jnp.asarray(np.concatenate([(a[:, None] > a[None, :]).astype(np.float32),
                                    np.ones((tq, LANES), np.float32)], axis=1), BF16)
    qb = C_SB // GROUP_W
    return pl.pallas_call(
        functools.partial(_sb_kernel, tq=tq),
        grid=(bsz, nq),
        in_specs=[
            pl.BlockSpec((tq, GROUP_W), lambda b, i: (b * nq + i, qb)),
            pl.BlockSpec((seq, GROUP_W), lambda b, i: (b, qb + 1)),
            pl.BlockSpec((seq, GROUP_W), lambda b, i: (b, qb + 2)),
            pl.BlockSpec((tq, tq + LANES), lambda b, i: (0, 0)),
        ],
        out_specs=pl.BlockSpec((tq, GROUP_W), lambda b, i: (b * nq + i, 0)),
        out_shape=jax.ShapeDtypeStruct((t, GROUP_W), F32),
        scratch_shapes=[pltpu.VMEM((N_GROUP_HEADS, tq, LANES), F32),
                        pltpu.VMEM((N_GROUP_HEADS, tq, LANES), F32)],
        compiler_params=_cparams(("parallel", "arbitrary")),
        name="sb_attn",
    )(main, main, main, u)


def _gla_kernel(qk_ref, v_ref, aux_ref, wup_ref, b_ref, lcum_ref, lsum_ref, eh_ref, bd_ref,
                o_ref, state_ref, *, tc):
    @pl.when(pl.program_id(1) == 0)
    def _():
        state_ref[...] = jnp.zeros_like(state_ref)

    c = GLA_CHUNK
    n = tc // c
    q = qk_ref[:, 0:LANES].astype(F32)
    k = qk_ref[:, LANES:2 * LANES].astype(F32)
    v = v_ref[...].astype(F32)
    a1, a2, a3 = _split3(aux_ref[...])
    w1 = wup_ref[0]
    w2 = wup_ref[1]
    logit = (_dot(a1, w1) + _dot(a2, w1) + _dot(a3, w1) + _dot(a1, w2) + _dot(a2, w2)
             + _dot(a1, wup_ref[2]) + b_ref[...])
    log_a = jax.nn.log_sigmoid(logit) * (1.0 / GLA_GATE_TAU)
    s1, s2, s3 = _split3(log_a)
    lcum = lcum_ref[...]
    lsum = lsum_ref[...]
    g = _dot(lcum, s1) + _dot(lcum, s2) + _dot(lcum, s3)
    gl = _dot(lsum, s1) + _dot(lsum, s2) + _dot(lsum, s3)
    q_in = q * jnp.exp(g)
    k_out = k * jnp.exp(gl - g)

    eh = eh_ref[...]
    g3 = g.reshape(n, c, LANES)
    k3 = k.reshape(n, c, LANES)
    q3 = q.reshape(n, c, LANES)
    v3 = v.reshape(n, c, GROUP_W)
    tpos = lax.broadcasted_iota(jnp.int32, (n, c, LANES), 1)
    o3 = jnp.zeros((n, c, GROUP_W), F32)
    for s in range(c):
        dec = jnp.exp(jnp.minimum(g3 - g3[:, s:s + 1, :], 0.0))
        pr = jnp.where(tpos >= s, q3 * k3[:, s:s + 1, :] * dec, 0.0)
        sc = _dot(pr.reshape(tc, LANES).astype(BF16), eh)
        o3 = o3 + sc.reshape(n, c, GROUP_W) * v3[:, s:s + 1, :]
    o_intra = o3.reshape(tc, GROUP_W)

    bd = bd_ref[...]
    st = state_ref[...]
    q_in_b = q_in.astype(BF16)
    k_out_b = k_out.astype(BF16)
    v_b = v_ref[...]
    kvs = [_dot_tn(v_b[ci * c:(ci + 1) * c, :], k_out_b[ci * c:(ci + 1) * c, :]) * bd for ci in range(n)]
    a_all = jnp.exp(gl)
    for ci in range(n):
        rs = slice(ci * c, (ci + 1) * c)
        o_ref[rs, :] = o_intra[rs, :] + _dot_nt(q_in_b[rs, :], st.astype(BF16))
        st = st * a_all[ci * c:ci * c + 1, :] + kvs[ci]
    state_ref[...] = st


def _gla_consts(tc):
    c = GLA_CHUNK
    a = np.arange(tc)
    same = (a[:, None] // c) == (a[None, :] // c)
    lcum = (same & (a[None, :] <= a[:, None])).astype(np.float32)
    lsum = same.astype(np.float32)
    hk = np.arange(LANES) // GLA_KEY_DIM
    hv = np.arange(GROUP_W) // HEAD_DIM
    eh = (hk[:, None] == hv[None, :]).astype(np.float32)
    return (jnp.asarray(lcum, BF16), jnp.asarray(lsum, BF16), jnp.asarray(eh, BF16),
            jnp.asarray(eh.T))


def _gla(main, aux, wup3, b_row, bsz, seq, tc=256):
    t = main.shape[0]
    nt = seq // tc
    lcum, lsum, eh, bd = _gla_consts(tc)
    const2 = lambda b, i: (0, 0)
    return pl.pallas_call(
        functools.partial(_gla_kernel, tc=tc),
        grid=(bsz, nt),
        in_specs=[
            pl.BlockSpec((tc, 2 * LANES), lambda b, i: (b * nt + i, C_GLA_QK // (2 * LANES))),
            pl.BlockSpec((tc, GROUP_W), lambda b, i: (b * nt + i, C_GLA_V // GROUP_W)),
            pl.BlockSpec((tc, LANES), lambda b, i: (b * nt + i, 0)),
            pl.BlockSpec((3, LANES, LANES), lambda b, i: (0, 0, 0)),
            pl.BlockSpec((1, LANES), const2),
            pl.BlockSpec((tc, tc), const2),
            pl.BlockSpec((tc, tc), const2),
            pl.BlockSpec((LANES, GROUP_W), const2),
            pl.BlockSpec((GROUP_W, LANES), const2),
        ],
        out_specs=pl.BlockSpec((tc, GROUP_W), lambda b, i: (b * nt + i, 0)),
        out_shape=jax.ShapeDtypeStruct((t, GROUP_W), F32),
        scratch_shapes=[pltpu.VMEM((GROUP_W, LANES), F32)],
        compiler_params=_cparams(("parallel", "arbitrary")),
        name="gla",
    )(main, main, aux, wup3, b_row, lcum, lsum, eh, bd)


def _mix_out_kernel(x_ref, osb_ref, ogla_ref, odil_ref, ofox_ref, r_ref, g_ref, eavg_ref, w_ref, xo_ref):
    eavg = eavg_ref[...]
    acc = x_ref[...]
    for gi, o_ref in enumerate((osb_ref, ogla_ref, odil_ref, ofox_ref)):
        o = o_ref[...]
        sq = o * o
        hi = sq.astype(BF16)
        lo = (sq - hi.astype(F32)).astype(BF16)
        ms = _dot(hi, eavg) + _dot(lo, eavg)
        y = o * lax.rsqrt(ms + NORM_EPS) * g_ref[:, gi * GROUP_W:(gi + 1) * GROUP_W]
        if gi == 1:
            r = r_ref[...].astype(F32)
            y = y * (r * jax.nn.sigmoid(r))
        acc = acc + _dot(y.astype(BF16), w_ref[gi * GROUP_W:(gi + 1) * GROUP_W, :])
    xo_ref[...] = acc


def _mix_out(x2, o_sb, o_gla, o_dil, o_fox, main, g_row, w_out, tm=512):
    t = x2.shape[0]
    hv = np.arange(GROUP_W) // HEAD_DIM
    eavg = jnp.asarray((hv[:, None] == hv[None, :]).astype(np.float32) / HEAD_DIM, BF16)
    ob = pl.BlockSpec((tm, GROUP_W), lambda i: (i, 0))
    return pl.pallas_call(
        _mix_out_kernel,
        grid=(t // tm,),
        in_specs=[
            pl.BlockSpec((tm, D_MODEL), lambda i: (i, 0)),
            ob, ob, ob, ob,
            pl.BlockSpec((tm, GROUP_W), lambda i: (i, C_GLA_R // GROUP_W)),
            pl.BlockSpec((1, D_MODEL), lambda i: (0, 0)),
            pl.BlockSpec((GROUP_W, GROUP_W), lambda i: (0, 0)),
            pl.BlockSpec((D_MODEL, D_MODEL), lambda i: (0, 0)),
        ],
        out_specs=pl.BlockSpec((tm, D_MODEL), lambda i: (i, 0)),
        out_shape=jax.ShapeDtypeStruct((t, D_MODEL), F32),
        compiler_params=_cparams(("parallel",)),
        name="mix_out",
    )(x2, o_sb, o_gla, o_dil, o_fox, main, g_row, eavg, w_out)


def _mixer_weights(w_in_l, w_gla_up_l, b_gla_l, b_fox_l, head_g_l):
    splits = (GROUP_W,) * 3 + (GLA_QK_W, GLA_QK_W, GROUP_W, GROUP_W, GLA_GATE_RANK) + (GROUP_W,) * 6 + (N_GROUP_HEADS,)
    cols = []
    start = 0
    for w in splits:
        cols.append(w_in_l[:, start:start + w])
        start += w
    (sb_q, sb_k, sb_v, gla_q, gla_k, gla_v, gla_r, gla_gd,
     dil_q, dil_k, dil_v, fox_q, fox_k, fox_v, fox_f) = cols
    hs = HEAD_DIM ** -0.5
    w_main = jnp.concatenate(
        [sb_q * hs, sb_k, sb_v, fox_q * hs, fox_k, fox_v, dil_q * hs, dil_k, dil_v,
         gla_q * (GLA_KEY_DIM ** -0.5), gla_k, gla_v, gla_r], axis=1).astype(BF16)
    w_aux = jnp.concatenate(
        [gla_gd, fox_f, jnp.zeros((D_MODEL, LANES - GLA_GATE_RANK - N_GROUP_HEADS), F32)], axis=1)
    wa_hi = w_aux.astype(BF16)
    wa_lo = (w_aux - wa_hi.astype(F32)).astype(BF16)
    wup = jnp.zeros((LANES, LANES), F32).at[:GLA_GATE_RANK, :].set(w_gla_up_l)
    wup3 = jnp.stack([p for p in _split3(wup)])
    b_gla_row = b_gla_l.reshape(1, LANES)
    b_fox_row = jnp.zeros((1, LANES), F32).at[0, AUX_FOX:AUX_FOX + N_GROUP_HEADS].set(b_fox_l)
    g_row = head_g_l.reshape(1, D_MODEL)
    return w_main, wa_hi, wa_lo, wup3, b_gla_row, b_fox_row, g_row


def _mixer_layer(x2, bsz, seq, norm_g, mw, w_out_bf, consts):
    w_main, wa_hi, wa_lo, wup3, b_gla_row, b_fox_row, g_row = mw
    fox_tab, dil_tab, dil_cq, dil_ck, dil_back = consts
    main, aux = _mix_in(x2, norm_g.reshape(1, D_MODEL), w_main, wa_hi, wa_lo)
    o_sb = _sb_attn(main, bsz, seq)
    o_gla = _gla(main, aux, wup3, b_gla_row, bsz, seq)
    o_dil = _attn(main, C_DIL, dil_cq, dil_ck, dil_tab, bsz, seq, False, dil_back)
    fcq, fck = _fox_prep(aux, b_fox_row, bsz, seq)
    o_fox = _attn(main, C_FOX, fcq, fck, fox_tab, bsz, seq, True, None)
    x2n = _mix_out(x2, o_sb, o_gla, o_dil, o_fox, main, g_row, w_out_bf)
    return x2n, (o_sb, o_gla, o_dil, o_fox)


def _top_rows(x, k):
    rows = []
    for _ in range(k):
        m = jnp.max(x, axis=0, keepdims=True)
        rows.append(m)
        x = jnp.where(x >= m, NEG, x)
    return rows


def _top_rows_rank(x, k):
    rows = []
    rank = jnp.full(x.shape, float(k), F32)
    for r in range(k):
        m = jnp.max(x, axis=0, keepdims=True)
        rows.append(m)
        hit = x >= m
        rank = jnp.where(hit, float(r), rank)
        x = jnp.where(hit, NEG, x)
    return rows, rank


_PEER_PAIRS = [(p, q) for p in range(PEER_TOPK) for q in range(PEER_TOPK) if (p + 1) * (q + 1) <= PEER_TOPK]
_PEER_CAND_ROWS = -(-len(_PEER_PAIRS) // 16) * 16


def _peer_a_kernel(x_ref, g_ref, wq_ref, keys_ref, seg_ref, xnt_ref, rank_ref, gb_ref, cnt_ref, ga_ref):
    xnt = _rms(x_ref[...], g_ref[...]).T.astype(BF16)
    xnt_ref[...] = xnt
    qt = _dot(wq_ref[...], xnt).astype(BF16)
    tm = xnt.shape[1]
    pad = _PEER_CAND_ROWS - len(_PEER_PAIRS)
    tw = BF16_LANES
    for h in range(PEER_HEADS):
        s1_all = _dot(keys_ref[2 * h], qt[(2 * h) * PEER_HALF:(2 * h + 1) * PEER_HALF, :])
        s2_all = _dot(keys_ref[2 * h + 1], qt[(2 * h + 1) * PEER_HALF:(2 * h + 2) * PEER_HALF, :])
        for tc in range(tm // tw):
            ts = slice(tc * tw, (tc + 1) * tw)
            st1 = s1_all[:, ts]
            st2 = s2_all[:, ts]
            a_rows = _top_rows(st1, PEER_TOPK)
            b_rows, rank2 = _top_rows_rank(st2, PEER_TOPK)
            cand = jnp.concatenate([a_rows[p] + b_rows[q] for p, q in _PEER_PAIRS]
                                   + [jnp.full((pad, tw), NEG, F32)], axis=0)
            best = _top_rows(cand, PEER_TOPK)
            zsum = jnp.zeros((1, tw), F32)
            for r in range(PEER_TOPK):
                zsum = zsum + jnp.exp(best[r] - best[0])
            sel = jnp.where(cand >= best[PEER_TOPK - 1], 1.0, 0.0).astype(BF16)
            cnt_rows = _dot(seg_ref[...], sel)
            cnt = jnp.zeros_like(st1)
            for p in range(PEER_TOPK):
                cnt = jnp.where(st1 == a_rows[p], cnt_rows[p:p + 1, :], cnt)
            rank_ref[h, :, ts] = rank2.astype(BF16)
            gb_ref[h, :, ts] = jnp.exp(st2 - b_rows[0]).astype(BF16)
            cnt_ref[h, :, ts] = cnt
            ga_ref[h, :, ts] = jnp.exp(st1 - (a_rows[0] + jnp.log(zsum)))


def _peer_a(x2, g_row, wq_t, keys, tm=512):
    t = x2.shape[0]
    seg = np.zeros((PEER_TOPK, _PEER_CAND_ROWS), np.float32)
    for r, (p, _) in enumerate(_PEER_PAIRS):
        seg[p, r] = 1.0
    sblk = pl.BlockSpec((PEER_HEADS, PEER_N_KEYS, tm), lambda i: (0, 0, i))
    return pl.pallas_call(
        _peer_a_kernel,
        grid=(t // tm,),
        in_specs=[
            pl.BlockSpec((tm, D_MODEL), lambda i: (i, 0)),
            pl.BlockSpec((1, D_MODEL), lambda i: (0, 0)),
            pl.BlockSpec((2 * PEER_HEADS * PEER_HALF, D_MODEL), lambda i: (0, 0)),
            pl.BlockSpec((2 * PEER_HEADS, PEER_N_KEYS, PEER_HALF), lambda i: (0, 0, 0)),
            pl.BlockSpec((PEER_TOPK, _PEER_CAND_ROWS), lambda i: (0, 0)),
        ],
        out_specs=[pl.BlockSpec((D_MODEL, tm), lambda i: (0, i)), sblk, sblk, sblk, sblk],
        out_shape=[
            jax.ShapeDtypeStruct((D_MODEL, t), BF16),
            jax.ShapeDtypeStruct((PEER_HEADS, PEER_N_KEYS, t), BF16),
            jax.ShapeDtypeStruct((PEER_HEADS, PEER_N_KEYS, t), BF16),
            jax.ShapeDtypeStruct((PEER_HEADS, PEER_N_KEYS, t), F32),
            jax.ShapeDtypeStruct((PEER_HEADS, PEER_N_KEYS, t), F32),
        ],
        compiler_params=_cparams(("parallel",)),
        name="peer_a",
    )(x2, g_row, wq_t, keys, jnp.asarray(seg, BF16))


def _gelu(a):
    return 0.5 * a * (1.0 + lax.erf(a * (2.0 ** -0.5)))


BF16_ROWS = 8
BF16_LANES = 2 * LANES


def _peer_b_kernel(xnt_ref, u_ref, vt_ref, rank_ref, gb_ref, cnt_ref, ga_ref, x_ref, o_ref,
                   acc_ref, act_ref, w_ref, cb_ref, ab_ref, *, te, tm, n_sub):
    ei = pl.program_id(1)
    n_i = te // PEER_N_KEYS
    zero_b = jnp.zeros((), BF16)

    @pl.when(ei == 0)
    def _():
        acc_ref[...] = jnp.zeros_like(acc_ref)

    n_rows = n_sub * n_i
    i0 = pl.multiple_of(ei * n_rows, n_rows)
    for h in range(PEER_HEADS):
        cblk = cnt_ref[h, pl.ds(i0, n_rows), :]
        ablk = ga_ref[h, pl.ds(i0, n_rows), :]
        for r in range(n_rows):
            cb_ref[r, h] = jnp.broadcast_to(cblk[r:r + 1, :], (BF16_ROWS, tm)).astype(BF16)
            ab_ref[r, h] = jnp.broadcast_to(ablk[r:r + 1, :], (BF16_ROWS, tm)).astype(BF16)

    def act_dot(s):
        act_ref[s] = _dot(u_ref[s * te:(s + 1) * te, :], xnt_ref[...])

    def weights(s):
        for il in range(n_i):
            for tc in range(tm // BF16_LANES):
                ts = slice(tc * BF16_LANES, (tc + 1) * BF16_LANES)
                gsum = [None] * (PEER_N_KEYS // BF16_ROWS)
                for h in range(PEER_HEADS):
                    cnt_b = cb_ref[s * n_i + il, h, :, ts]
                    ga_b = ab_ref[s * n_i + il, h, :, ts]
                    for jg in range(PEER_N_KEYS // BF16_ROWS):
                        js = slice(jg * BF16_ROWS, (jg + 1) * BF16_ROWS)
                        term = jnp.where(rank_ref[h, js, ts] < cnt_b, gb_ref[h, js, ts] * ga_b, zero_b)
                        gsum[jg] = term if gsum[jg] is None else gsum[jg] + term
                for jg in range(PEER_N_KEYS // BF16_ROWS):
                    rows = slice(il * PEER_N_KEYS + jg * BF16_ROWS, il * PEER_N_KEYS + (jg + 1) * BF16_ROWS)
                    w_ref[s, rows, ts] = gsum[jg] * _gelu(act_ref[s, rows, ts]).astype(BF16)

    def out_dot(s):
        acc_ref[...] += _dot(vt_ref[:, s * te:(s + 1) * te], w_ref[s])

    for k in range(n_sub):
        act_dot(k)
    for k in range(n_sub):
        weights(k)
        out_dot(k)

    @pl.when(ei == pl.num_programs(1) - 1)
    def _():
        o_ref[...] = x_ref[...] + acc_ref[...].T


def _peer_b(x2, xnt, u_bf, vt_bf, rank, gb, cnt, ga, tm=512, te=1024, n_sub=2):
    t = x2.shape[0]
    sblk = pl.BlockSpec((PEER_HEADS, PEER_N_KEYS, tm), lambda i, e: (0, 0, i))
    return pl.pallas_call(
        functools.partial(_peer_b_kernel, te=te, tm=tm, n_sub=n_sub),
        grid=(t // tm, PEER_N_EXPERTS // (te * n_sub)),
        in_specs=[
            pl.BlockSpec((D_MODEL, tm), lambda i, e: (0, i)),
            pl.BlockSpec((te * n_sub, D_MODEL), lambda i, e: (e, 0)),
            pl.BlockSpec((D_MODEL, te * n_sub), lambda i, e: (0, e)),
            sblk, sblk, sblk, sblk,
            pl.BlockSpec((tm, D_MODEL), lambda i, e: (i, 0)),
        ],
        out_specs=pl.BlockSpec((tm, D_MODEL), lambda i, e: (i, 0)),
        out_shape=jax.ShapeDtypeStruct((t, D_MODEL), F32),
        scratch_shapes=[
            pltpu.VMEM((D_MODEL, tm), F32),
            pltpu.VMEM((n_sub, te, tm), F32),
            pltpu.VMEM((n_sub, te, tm), BF16),
            pltpu.VMEM((n_sub * te // PEER_N_KEYS, PEER_HEADS, BF16_ROWS, tm), BF16),
            pltpu.VMEM((n_sub * te // PEER_N_KEYS, PEER_HEADS, BF16_ROWS, tm), BF16),
        ],
        compiler_params=_cparams(("parallel", "arbitrary")),
        name="peer_b",
    )(xnt, u_bf, vt_bf, rank, gb, cnt, ga, x2)


def _final_norm_kernel(x_ref, g_ref, o_ref):
    o_ref[...] = _rms(x_ref[...], g_ref[...])


def _final_norm(x2, g_row, tm=1024):
    t = x2.shape[0]
    return pl.pallas_call(
        _final_norm_kernel,
        grid=(t // tm,),
        in_specs=[pl.BlockSpec((tm, D_MODEL), lambda i: (i, 0)), pl.BlockSpec((1, D_MODEL), lambda i: (0, 0))],
        out_specs=pl.BlockSpec((tm, D_MODEL), lambda i: (i, 0)),
        out_shape=jax.ShapeDtypeStruct((t, D_MODEL), F32),
        compiler_params=_cparams(("parallel",)),
        name="final_norm",
    )(x2, g_row)


def _peer_layer(x2, norm_g, w_peer_q_l, sub_keys_l, peer_u_l, peer_v_l):
    wq_t = w_peer_q_l.T.astype(BF16)
    keys = sub_keys_l.reshape(2 * PEER_HEADS, PEER_N_KEYS, PEER_HALF).astype(BF16)
    xnt, rank, gb, cnt, ga = _peer_a(x2, norm_g.reshape(1, D_MODEL), wq_t, keys)
    return _peer_b(x2, xnt, peer_u_l.astype(BF16), peer_v_l.T.astype(BF16), rank, gb, cnt, ga)


def kernel(x, norm_mix_g, w_in, w_gla_up, b_gla, b_fox, head_norm_g, w_out, norm_ffn_g,
           w_peer_q, peer_sub_keys, peer_u, peer_v, norm_final_g):
    bsz, seq, _ = x.shape
    depth = w_in.shape[0]
    tq = 256
    dil_tab, dil_cq, dil_ck, dil_back = _dil_consts(seq, tq)
    consts = (_fox_table(tq), dil_tab, dil_cq, dil_ck, dil_back)
    x2 = x.reshape(bsz * seq, D_MODEL)
    for l in range(depth):
        mw = _mixer_weights(w_in[l], w_gla_up[l], b_gla[l], b_fox[l], head_norm_g[l])
        x2, _ = _mixer_layer(x2, bsz, seq, norm_mix_g[l], mw, w_out[l].astype(BF16), consts)
        x2 = _peer_layer(x2, norm_ffn_g[l], w_peer_q[l], peer_sub_keys[l], peer_u[l], peer_v[l])
    return _final_norm(x2, norm_final_g.reshape(1, D_MODEL)).reshape(bsz, seq, D_MODEL)
```
